```python
import jax, jax.numpy as jnp
from jax import lax
import numpy as np

D_MODEL = 1024
BATCH = 4
SEQ = 4096
DEPTH = 2
DEC_BATCH = 128
DEC_SEQ = 4
PAST_LEN = 16384
PAGE_SIZE = 128

WINDOW = 128
H_A = 4
KVH_A = 2
GQ_A = H_A // KVH_A
HD_A = 64
ALIBI_MAX_EXP = 8.0
H_B = 4
DK_B = 64
DV_B = 128
RET_CHUNK = 128
H_C = 4
DK_C = 64
DV_C = 64
HGRN_CHUNK = 64
N_MEM = 256
XH = 4
XHD = 128
DFF = 2816
N_EXPERTS = 8
TOP_K = 2
DFF_E = 3584
MOE_BLOCK = 256
EPS = 1e-6
PROJ_SIZES = (H_A * HD_A, KVH_A * HD_A, KVH_A * HD_A,
              H_B * DK_B, H_B * DK_B, H_B * DV_B, H_B * DV_B,
              H_C * DK_C, H_C * DK_C, H_C * DV_C, H_C * DV_C)
D_PROJ = 3072
D_MIX = H_A * HD_A + H_B * DV_B + H_C * DV_C

kernel_name = 'hybrid_swa_retnet_hgrn2_moe_decoder_step'


def _rmsnorm(x, g):
    xf = x.astype(jnp.float32)
    y = xf * lax.rsqrt(jnp.mean(xf * xf, axis=-1, keepdims=True) + EPS)
    return (y * g.astype(jnp.float32)).astype(x.dtype)


def _group_norm(o):
    of = o.astype(jnp.float32)
    c = of - jnp.mean(of, axis=-1, keepdims=True)
    return c * lax.rsqrt(jnp.mean(c * c, axis=-1, keepdims=True) + EPS)


def _split_proj(p):
    outs, off = [], 0
    for n in PROJ_SIZES:
        outs.append(p[..., off:off + n])
        off += n
    return outs


def _sink_window_attention(q, k, v, rel, sinks, slopes):
    s = jnp.einsum('...ikgd,...jkd->...kgij', q, k).astype(jnp.float32) * (HD_A ** -0.5)
    relf = rel.astype(jnp.float32)[..., None, None, :, :]
    allowed = ((rel >= 0) & (rel < WINDOW))[..., None, None, :, :]
    s = jnp.where(allowed, s - slopes[:, :, None, None] * relf, -jnp.inf)
    sink = jnp.broadcast_to(sinks.astype(jnp.float32)[:, :, None, None], s.shape[:-1] + (1,))
    p = jax.nn.softmax(jnp.concatenate([s, sink], axis=-1), axis=-1)[..., :-1]
    return jnp.einsum('...kgij,...jkd->...ikgd', p.astype(v.dtype), v)


def _swa_banded(q, k, v, sinks, slopes):
    Bn, T = q.shape[0], q.shape[1]
    nb = T // WINDOW
    qb = q.reshape(Bn, nb, WINDOW, KVH_A, GQ_A, HD_A)

    def band(x):
        xp = jnp.concatenate([jnp.zeros_like(x[:, :WINDOW]), x], axis=1)
        xp = xp.reshape(Bn, nb + 1, WINDOW, KVH_A, HD_A)
        return jnp.concatenate([xp[:, :-1], xp[:, 1:]], axis=2)

    blk = jnp.arange(nb)[:, None, None]
    i = jnp.arange(WINDOW)[None, :, None]
    j = jnp.arange(2 * WINDOW)[None, None, :]
    kpos = (blk - 1) * WINDOW + j
    rel = jnp.where(kpos >= 0, WINDOW + i - j, WINDOW)
    o = _sink_window_attention(qb, band(k), band(v), rel, sinks, slopes)
    return o.reshape(Bn, T, H_A * HD_A)


def _chunks(x, c):
    Bn, T, H, d = x.shape
    return x.reshape(Bn, T // c, c, H, d).transpose(1, 0, 3, 2, 4)


def _unchunk(y):
    n, Bn, H, c, d = y.shape
    return y.transpose(1, 0, 3, 2, 4).reshape(Bn, n * c, H, d)


def _retention(q, k, v, s0):
    f32 = jnp.float32
    T = q.shape[1]
    c = RET_CHUNK if T % RET_CHUNK == 0 else T
    log_g = jnp.log1p(-jnp.exp2(-5.0 - jnp.arange(H_B, dtype=f32)))
    idx = jnp.arange(c, dtype=f32)
    diff = idx[:, None] - idx[None, :]
    dmat = jnp.where(diff >= 0, jnp.exp(log_g[:, None, None] * jnp.maximum(diff, 0.0)), 0.0)
    q_dec = jnp.exp(log_g[:, None] * (idx + 1.0))[..., None]
    k_dec = jnp.exp(log_g[:, None] * (c - 1.0 - idx))[..., None]
    c_dec = jnp.exp(log_g * c)[:, None, None]

    def step(S, blk):
        qb, kb, vb = blk
        a = jnp.einsum('bhid,bhjd->bhij', qb, kb) * dmat
        o = jnp.einsum('bhij,bhjv->bhiv', a, vb) + jnp.einsum('bhid,bhdv->bhiv', qb * q_dec, S)
        S = S * c_dec + jnp.einsum('bhjd,bhjv->bhdv', kb * k_dec, vb)
        return S, o

    S, o = lax.scan(step, s0.astype(f32),
                    (_chunks(q.astype(f32), c), _chunks(k.astype(f32), c), _chunks(v.astype(f32), c)))
    return _unchunk(o), S


def _hgrn2(q, k, log_f, v, s0):
    f32 = jnp.float32
    T = q.shape[1]
    c = HGRN_CHUNK if T % HGRN_CHUNK == 0 else T
    causal = jnp.tril(jnp.ones((c, c), dtype=bool))[:, :, None]

    def step(S, blk):
        qb, kb, gb, vb = blk
        b = jnp.cumsum(gb, axis=2)
        w = jnp.exp(jnp.where(causal, b[:, :, :, None, :] - b[:, :, None, :, :], -jnp.inf))
        a = jnp.einsum('bhid,bhijd->bhij', qb, w * kb[:, :, None, :, :])
        o = jnp.einsum('bhij,bhjv->bhiv', a, vb) + jnp.einsum('bhid,bhdv->bhiv', qb * jnp.exp(b), S)
        b_last = b[:, :, -1:, :]
        S = jnp.exp(b_last[:, :, 0, :])[..., None] * S + jnp.einsum('bhjd,bhjv->bhdv', kb * jnp.exp(b_last - b), vb)
        return S, o

    S, o = lax.scan(step, s0.astype(f32),
                    (_chunks(q.astype(f32), c), _chunks(k.astype(f32), c),
                     _chunks(log_f.astype(f32), c), _chunks(v.astype(f32), c)))
    return _unchunk(o), S


def _token_mixers(h, w_in, w_out, sinks, slopes, lb, hgrn_gain, ret_s0, hgrn_s0, k_buf, v_buf):
    f32 = jnp.float32
    Bn, T, _ = h.shape
    dt = h.dtype
    qa, ka, va, qb, kb, vb, gb, fc, qc, ic, gc = _split_proj(h @ w_in)
    qa = qa.reshape(Bn, T, KVH_A, GQ_A, HD_A)
    ka = ka.reshape(Bn, T, KVH_A, HD_A)
    va = va.reshape(Bn, T, KVH_A, HD_A)
    if k_buf is None:
        oa = _swa_banded(qa, ka, va, sinks, slopes)
        k_keep, v_keep = ka[:, T - WINDOW:], va[:, T - WINDOW:]
    else:
        wb = k_buf.shape[1]
        k_all = jnp.concatenate([k_buf.astype(dt), ka], axis=1)
        v_all = jnp.concatenate([v_buf.astype(dt), va], axis=1)
        rel = jnp.arange(T)[:, None] - jnp.arange(-wb, T)[None, :]
        oa = _sink_window_attention(qa, k_all, v_all, rel, sinks, slopes).reshape(Bn, T, H_A * HD_A)
        k_keep, v_keep = k_all[:, T:], v_all[:, T:]
    ob, ret_s = _retention(qb.reshape(Bn, T, H_B, DK_B), kb.reshape(Bn, T, H_B, DK_B) * (DK_B ** -0.5),
                           vb.reshape(Bn, T, H_B, DV_B), ret_s0)
    ob = (_group_norm(ob) * jax.nn.silu(gb.astype(f32)).reshape(Bn, T, H_B, DV_B)).astype(dt)
    ob = ob.reshape(Bn, T, H_B * DV_B)
    lbh = lb.reshape(H_C, DK_C)
    f = lbh + (1.0 - lbh) * jax.nn.sigmoid(fc.astype(f32).reshape(Bn, T, H_C, DK_C))
    oc, hgrn_s = _hgrn2(jax.nn.silu(qc.astype(f32)).reshape(Bn, T, H_C, DK_C), 1.0 - f, jnp.log(f),
                        ic.reshape(Bn, T, H_C, DV_C), hgrn_s0)
    oc = (_rmsnorm(oc, hgrn_gain.reshape(H_C, DV_C)) * jax.nn.silu(gc.astype(f32)).reshape(Bn, T, H_C, DV_C)).astype(dt)
    oc = oc.reshape(Bn, T, H_C * DV_C)
    out = jnp.concatenate([oa, ob, oc], axis=-1) @ w_out
    return out, (k_keep, v_keep, ret_s.astype(ret_s0.dtype), hgrn_s.astype(hgrn_s0.dtype))


def _cross_attention(h, mk, mv, wq, wo):
    Bn, T, _ = h.shape
    q = (h @ wq).reshape(Bn, T, XH, XHD)
    s = jnp.einsum('bthd,bmhd->bhtm', q, mk.astype(q.dtype)).astype(jnp.float32) * (XHD ** -0.5)
    p = jax.nn.softmax(s, axis=-1).astype(q.dtype)
    o = jnp.einsum('bhtm,bmhd->bthd', p, mv.astype(q.dtype)).reshape(Bn, T, XH * XHD)
    return o @ wo


def _swiglu(h, w1, w3, w2):
    return (jax.nn.silu(h @ w1) * (h @ w3)) @ w2


def _moe_swiglu(x, w_router, w1, w3, w2):
    shape = x.shape
    xf = x.reshape(-1, D_MODEL)
    n = xf.shape[0]
    logits = jnp.dot(xf.astype(jnp.float32), w_router.astype(jnp.float32))
    top_v, top_i = lax.top_k(logits, TOP_K)
    gate = jax.nn.softmax(top_v, axis=-1).astype(x.dtype)
    a = n * TOP_K
    e_flat = top_i.reshape(a)
    tok_flat = jnp.repeat(jnp.arange(n, dtype=jnp.int32), TOP_K)
    order = jnp.argsort(e_flat)
    e_sorted = e_flat[order]
    counts = jnp.bincount(e_flat, length=N_EXPERTS)
    padded = (counts + MOE_BLOCK - 1) // MOE_BLOCK * MOE_BLOCK
    start = jnp.cumsum(counts) - counts
    pend = jnp.cumsum(padded)
    pstart = pend - padded
    dest = pstart[e_sorted] + jnp.arange(a) - start[e_sorted]
    n_blocks = -(-a // MOE_BLOCK) + N_EXPERTS
    rows = n_blocks * MOE_BLOCK
    tok_buf = jnp.full((rows,), n, jnp.int32).at[dest].set(tok_flat[order])
    gate_buf = jnp.zeros((rows,), x.dtype).at[dest].set(gate.reshape(a)[order])
    block_e = jnp.minimum(jnp.searchsorted(pend, jnp.arange(n_blocks) * MOE_BLOCK, side='right'), N_EXPERTS - 1)
    x_pad = jnp.concatenate([xf, jnp.zeros((1, D_MODEL), xf.dtype)], axis=0)
    xb = x_pad[tok_buf].reshape(n_blocks, MOE_BLOCK, D_MODEL)

    def expert_block(args):
        xe, e = args
        return (jax.nn.silu(xe @ w1[e]) * (xe @ w3[e])) @ w2[e]

    yb = lax.map(expert_block, (xb, block_e)).reshape(rows, D_MODEL)
    y = jnp.zeros((n + 1, D_MODEL), x.dtype).at[tok_buf].add(yb * gate_buf[:, None])
    return y[:n].reshape(shape)


def setup_inputs(seed: int = 0) -> dict:
    key = jax.random.key(seed)
    ks = iter(jax.random.split(key, 40))
    f32 = jnp.float32
    n_dense = (DEPTH + 1) // 2
    n_moe = DEPTH // 2
    win_buf = min(WINDOW, PAST_LEN)

    def nrm(shape, scale):
        return jax.random.normal(next(ks), shape, f32) * scale

    def gain(shape):
        return 1.0 + nrm(shape, 0.05)

    return {
        'x_prompt': nrm((BATCH, SEQ, D_MODEL), 1.0),
        'x_sample': nrm((DEC_BATCH, DEC_SEQ, D_MODEL), 1.0),
        'cache_swa_k': nrm((DEPTH, DEC_BATCH, win_buf, KVH_A, HD_A), 1.0),
        'cache_swa_v': nrm((DEPTH, DEC_BATCH, win_buf, KVH_A, HD_A), 1.0),
        'state_ret': nrm((DEPTH, DEC_BATCH, H_B, DK_B, DV_B), 0.3),
        'state_hgrn': nrm((DEPTH, DEC_BATCH, H_C, DK_C, DV_C), 0.3),
        'cache_mem_k': nrm((DEPTH, DEC_BATCH, N_MEM, XH, XHD), 1.0),
        'cache_mem_v': nrm((DEPTH, DEC_BATCH, N_MEM, XH, XHD), 1.0),
        'mem_prompt': nrm((BATCH, N_MEM, D_MODEL), 1.0),
        'norm_mix': gain((DEPTH, D_MODEL)),
        'w_in': nrm((DEPTH, D_MODEL, D_PROJ), D_MODEL ** -0.5),
        'swa_sinks': nrm((DEPTH, H_A), 0.5),
        'hgrn_lb_logits': nrm((DEPTH, H_C * DK_C), 0.5),
        'hgrn_norm': gain((DEPTH, H_C * DV_C)),
        'w_out': nrm((DEPTH, D_MIX, D_MODEL), D_MIX ** -0.5),
        'norm_xattn': gain((DEPTH, D_MODEL)),
        'norm_mem': gain((DEPTH, D_MODEL)),
        'wx_q': nrm((DEPTH, D_MODEL, XH * XHD), D_MODEL ** -0.5),
        'wx_k': nrm((DEPTH, D_MODEL, XH * XHD), D_MODEL ** -0.5),
        'wx_v': nrm((DEPTH, D_MODEL, XH * XHD), D_MODEL ** -0.5),
        'wx_o': nrm((DEPTH, XH * XHD, D_MODEL), (XH * XHD) ** -0.5),
        'norm_ffn': gain((DEPTH, D_MODEL)),
        'ffn_w1': nrm((n_dense, D_MODEL, DFF), D_MODEL ** -0.5),
        'ffn_w3': nrm((n_dense, D_MODEL, DFF), D_MODEL ** -0.5),
        'ffn_w2': nrm((n_dense, DFF, D_MODEL), DFF ** -0.5),
        'moe_router': nrm((n_moe, D_MODEL, N_EXPERTS), D_MODEL ** -0.5),
        'moe_w1': nrm((n_moe, N_EXPERTS, D_MODEL, DFF_E), D_MODEL ** -0.5),
        'moe_w3': nrm((n_moe, N_EXPERTS, D_MODEL, DFF_E), D_MODEL ** -0.5),
        'moe_w2': nrm((n_moe, N_EXPERTS, DFF_E, D_MODEL), DFF_E ** -0.5),
        'final_norm': gain((D_MODEL,)),
    }


def reference(x_prompt, x_sample, cache_swa_k, cache_swa_v, state_ret, state_hgrn, cache_mem_k, cache_mem_v,
              mem_prompt, norm_mix, w_in, swa_sinks, hgrn_lb_logits, hgrn_norm, w_out,
              norm_xattn, norm_mem, wx_q, wx_k, wx_v, wx_o, norm_ffn, ffn_w1, ffn_w3, ffn_w2,
              moe_router, moe_w1, moe_w3, moe_w2, final_norm):
    f32 = jnp.float32
    slopes = jnp.exp2(-(ALIBI_MAX_EXP / H_A) * jnp.arange(1, H_A + 1, dtype=f32)).reshape(KVH_A, GQ_A)
    lb_p = jax.nn.softmax(hgrn_lb_logits.astype(f32), axis=0)
    lower_bounds = jnp.cumsum(lb_p, axis=0) - lb_p[0]
    xp, xs = x_prompt, x_sample
    Bp, Bs = xp.shape[0], xs.shape[0]
    p_k, p_v, p_r, p_c, p_mk, p_mv = [], [], [], [], [], []
    s_k, s_v, s_r, s_c = [], [], [], []
    for l in range(DEPTH):
        sinks = swa_sinks[l].reshape(KVH_A, GQ_A)
        zr = jnp.zeros((Bp, H_B, DK_B, DV_B), xp.dtype)
        zc = jnp.zeros((Bp, H_C, DK_C, DV_C), xp.dtype)
        mp, (pk, pv, pr, pc) = _token_mixers(_rmsnorm(xp, norm_mix[l]), w_in[l], w_out[l], sinks, slopes,
                                             lower_bounds[l], hgrn_norm[l], zr, zc, None, None)
        ms, (sk, sv, sr, sc) = _token_mixers(_rmsnorm(xs, norm_mix[l]), w_in[l], w_out[l], sinks, slopes,
                                             lower_bounds[l], hgrn_norm[l], state_ret[l], state_hgrn[l],
                                             cache_swa_k[l], cache_swa_v[l])
        xp = xp + mp
        xs = xs + ms
        memn = _rmsnorm(mem_prompt, norm_mem[l])
        mk = (memn @ wx_k[l]).reshape(Bp, N_MEM, XH, XHD)
        mv = (memn @ wx_v[l]).reshape(Bp, N_MEM, XH, XHD)
        xp = xp + _cross_attention(_rmsnorm(xp, norm_xattn[l]), mk, mv, wx_q[l], wx_o[l])
        xs = xs + _cross_attention(_rmsnorm(xs, norm_xattn[l]), cache_mem_k[l], cache_mem_v[l], wx_q[l], wx_o[l])
        hp = _rmsnorm(xp, norm_ffn[l])
        hs = _rmsnorm(xs, norm_ffn[l])
        e = l // 2
        if l % 2 == 0:
            xp = xp + _swiglu(hp, ffn_w1[e], ffn_w3[e], ffn_w2[e])
            xs = xs + _swiglu(hs, ffn_w1[e], ffn_w3[e], ffn_w2[e])
        else:
            xp = xp + _moe_swiglu(hp, moe_router[e], moe_w1[e], moe_w3[e], moe_w2[e])
            xs = xs + _moe_swiglu(hs, moe_router[e], moe_w1[e], moe_w3[e], moe_w2[e])
        p_k.append(pk); p_v.append(pv); p_r.append(pr); p_c.append(pc)
        p_mk.append(mk); p_mv.append(mv)
        s_k.append(sk); s_v.append(sv); s_r.append(sr); s_c.append(sc)
    y_prompt = _rmsnorm(xp, final_norm)
    y_sample = _rmsnorm(xs, final_norm)
    return (y_prompt, y_sample,
            jnp.stack(p_k), jnp.stack(p_v), jnp.stack(p_r), jnp.stack(p_c), jnp.stack(p_mk), jnp.stack(p_mv),
            jnp.stack(s_k), jnp.stack(s_v), jnp.stack(s_r), jnp.stack(s_c))
```

```python
import functools

import numpy as np
import jax
import jax.numpy as jnp
from jax import lax
from jax.experimental import pallas as pl
from jax.experimental.pallas import tpu as pltpu

F32, BF16, I32 = jnp.float32, jnp.bfloat16, jnp.int32
HIGHEST = lax.Precision.HIGHEST
NEG_INF = float("-inf")

D_MODEL = 1024
WINDOW = 128
H_A, KVH_A, GQ_A, HD_A = 4, 2, 2, 64
ALIBI_MAX_EXP = 8.0
H_B, DK_B, DV_B = 4, 64, 128
RET_CHUNK = 128
H_C, DK_C, DV_C = 4, 64, 64
N_MEM, XH, XHD = 256, 4, 128
N_EXPERTS, TOP_K = 8, 2
EPS = 1e-6
D_PROJ = 3072
C_QA, C_KA, C_VA = 0, 256, 384
C_QB, C_KB, C_VB, C_GB = 512, 768, 1024, 1536
C_FC, C_QC, C_IC, C_GC = 2048, 2304, 2560, 2816
HC = H_C * DK_C

BLK = 128
SUB_P = 16
T_PAD = 8
GB = BLK // T_PAD
VMEM_LIMIT = 56 * 1024 * 1024


def _pick(n, cands):
    for c in cands:
        if n % c == 0:
            return c
    raise ValueError(f"no tile for {n}")


def _cparams(*sem):
    return pltpu.CompilerParams(dimension_semantics=sem, vmem_limit_bytes=VMEM_LIMIT)


def _mm(a, b):
    return jnp.dot(a.astype(BF16), b.astype(BF16), preferred_element_type=F32)


def _mm_nt(a, b):
    return lax.dot_general(a.astype(BF16), b.astype(BF16), (((1,), (1,)), ((), ())),
                           preferred_element_type=F32)


def _mm_hi(a, b):
    return jnp.dot(a, b, precision=HIGHEST, preferred_element_type=F32)


def _sigmoid(x):
    return 1.0 / (1.0 + jnp.exp(-x))


def _silu(x):
    return x * _sigmoid(x)


def _rms(x, g):
    return x * lax.rsqrt(jnp.mean(x * x, axis=-1, keepdims=True) + EPS) * g


def _rms_matmul_kernel(x_ref, g_ref, w_ref, o_ref, xn_ref):
    @pl.when(pl.program_id(1) == 0)
    def _():
        xn_ref[...] = _rms(x_ref[...], g_ref[...]).astype(BF16)

    o_ref[...] = jnp.dot(xn_ref[...], w_ref[...].astype(BF16), preferred_element_type=F32)


def _rms_matmul(x, g, w, row_off=0, rows=None):
    n, k = x.shape
    m = w.shape[1]
    rows = n if rows is None else rows
    tm = _pick(np.gcd(rows, row_off) if row_off else rows, (512, 256, 128, 64, 32, 16, 8))
    tn = _pick(m, (512, 256, 128))
    off = row_off // tm
    return pl.pallas_call(
        _rms_matmul_kernel,
        grid=(rows // tm, m // tn),
        in_specs=[pl.BlockSpec((tm, k), lambda i, j: (i + off, 0)),
                  pl.BlockSpec((1, k), lambda i, j: (0, 0)),
                  pl.BlockSpec((k, tn), lambda i, j: (0, j))],
        out_specs=pl.BlockSpec((tm, tn), lambda i, j: (i, j)),
        out_shape=jax.ShapeDtypeStruct((rows, m), F32),
        scratch_shapes=[pltpu.VMEM((tm, k), BF16)],
        compiler_params=_cparams("parallel", "arbitrary"),
        name="rms_matmul",
    )(x, g.reshape(1, k), w)


def _matmul_res_kernel(a_ref, w_ref, x_ref, o_ref):
    o_ref[...] = x_ref[...] + _mm(a_ref[...], w_ref[...])


def _matmul_res(a, w, x, row_off):
    rows, k = a.shape
    m = w.shape[1]
    tm = _pick(np.gcd(rows, row_off) if row_off else rows, (512, 256, 128, 64, 32, 16, 8))
    tn = _pick(m, (512, 256, 128))
    off = row_off // tm
    return pl.pallas_call(
        _matmul_res_kernel,
        grid=(rows // tm, m // tn),
        in_specs=[pl.BlockSpec((tm, k), lambda i, j: (i, 0)),
                  pl.BlockSpec((k, tn), lambda i, j: (0, j)),
                  pl.BlockSpec((tm, tn), lambda i, j: (i + off, j))],
        out_specs=pl.BlockSpec((tm, tn), lambda i, j: (i + off, j)),
        out_shape=jax.ShapeDtypeStruct(x.shape, F32),
        input_output_aliases={2: 0},
        compiler_params=_cparams("parallel", "arbitrary"),
        name="matmul_res",
    )(a, w, x)


def _rmsnorm_kernel(x_ref, g_ref, o_ref):
    o_ref[...] = _rms(x_ref[...], g_ref[...])


def _rmsnorm_rows(x, g, row_off, rows):
    n, k = x.shape
    tm = _pick(np.gcd(rows, row_off) if row_off else rows, (512, 256, 128, 64, 32, 16, 8))
    off = row_off // tm
    return pl.pallas_call(
        _rmsnorm_kernel,
        grid=(rows // tm,),
        in_specs=[pl.BlockSpec((tm, k), lambda i: (i + off, 0)),
                  pl.BlockSpec((1, k), lambda i: (0, 0))],
        out_specs=pl.BlockSpec((tm, k), lambda i: (i, 0)),
        out_shape=jax.ShapeDtypeStruct((rows, k), F32),
        compiler_params=_cparams("parallel"),
        name="final_rmsnorm",
    )(x, g.reshape(1, k))


def _retention_gamma():
    return 1.0 - np.exp2(-5.0 - np.arange(H_B, dtype=np.float64))


def _alibi_slopes():
    return np.exp2(-(ALIBI_MAX_EXP / H_A) * np.arange(1, H_A + 1, dtype=np.float64))


def _head_cols(vals_hr, width):
    return np.repeat(vals_hr.T[:, :, None], width, axis=2).reshape(vals_hr.shape[1], -1)


def _hgrn_tables(sub):
    nsub = BLK // sub
    r = np.arange(BLK)
    same = (r[:, None] // sub) == (r[None, :] // sub)
    tril = (same & (r[None, :] <= r[:, None])).astype(np.float32)
    lastsel = (r[None, :] == (r[:, None] // sub) * sub + sub - 1).astype(np.float32)
    pr = np.arange(nsub * sub * sub)
    pi_, pj_, pI = (pr // sub) % sub, pr % sub, pr // (sub * sub)
    repi = (r[None, :] == (pI * sub + pi_)[:, None]).astype(np.float32)
    repi_masked = repi * (pj_ <= pi_)[:, None]
    sel = repi.T
    hh = np.arange(HC) // DK_C
    bd = (hh[:, None] == hh[None, :]).astype(np.float32)
    return dict(tril=jnp.asarray(tril), lastsel=jnp.asarray(lastsel), repi=jnp.asarray(repi),
                repim=jnp.asarray(repi_masked, dtype=BF16), sel=jnp.asarray(sel, dtype=BF16),
                onesbd=jnp.asarray(bd, dtype=BF16), bdf=jnp.asarray(bd))


def _prompt_tables():
    gam = _retention_gamma()
    i = np.arange(RET_CHUNK, dtype=np.float64)
    diff = i[:, None] - i[None, :]
    dmat = np.where(diff >= 0, gam[:, None, None] ** np.maximum(diff, 0.0), 0.0)
    qdec = _head_cols(gam[:, None] ** (i + 1.0), DK_B)
    kdec = _head_cols(gam[:, None] ** (RET_CHUNK - 1.0 - i), DK_B) * (DK_B ** -0.5)
    cdec = gam ** RET_CHUNK
    slopes = _alibi_slopes().reshape(KVH_A, GQ_A)
    qi = np.tile(np.arange(WINDOW), GQ_A)[:, None]
    qg = np.repeat(np.arange(GQ_A), WINDOW)[:, None]
    j = np.arange(2 * WINDOW)[None, :]
    rel = WINDOW + qi - j
    ok = (rel >= 0) & (rel < WINDOW)
    bias = np.stack([np.where(ok, -slopes[kh][qg] * rel, NEG_INF) for kh in range(KVH_A)])
    bias_first = np.where(j < WINDOW, NEG_INF, bias)
    return dict(dmat=jnp.asarray(dmat, dtype=F32), qdec=jnp.asarray(qdec, dtype=F32),
                kdec=jnp.asarray(kdec, dtype=F32), cdec=cdec,
                bias=jnp.asarray(np.stack([bias_first, bias]), dtype=F32))


def _sample_tables(t):
    gam = _retention_gamma()
    r = np.arange(BLK)
    ti, bi = r % T_PAD, r // T_PAD
    diff = (ti[:, None] - ti[None, :]).astype(np.float64)
    same = bi[:, None] == bi[None, :]
    dblk = np.where(same & (diff >= 0), gam[:, None, None] ** np.maximum(diff, 0.0), 0.0)
    qdec = _head_cols(gam[:, None] ** (ti + 1.0), DK_B)
    kdec = _head_cols(gam[:, None] ** (t - 1.0 - ti), DK_B) * (DK_B ** -0.5)
    cdec = gam ** t
    slopes = _alibi_slopes().reshape(KVH_A, GQ_A)
    qt = np.tile(np.arange(T_PAD), GQ_A)[:, None]
    qg = np.repeat(np.arange(GQ_A), T_PAD)[:, None]
    c = np.arange(2 * WINDOW)[None, :]
    rel = qt + WINDOW - c
    ok = (rel >= 0) & (rel < WINDOW) & (c < WINDOW + t)
    bias = np.stack([np.where(ok, -slopes[kh][qg] * rel, NEG_INF) for kh in range(KVH_A)])
    padmask = np.repeat((ti < t).astype(np.float32)[:, None], HC, axis=1)
    return dict(dblk=jnp.asarray(dblk, dtype=F32), qdec=jnp.asarray(qdec, dtype=F32),
                kdec=jnp.asarray(kdec, dtype=F32), cdec=cdec, bias=jnp.asarray(bias, dtype=F32),
                padmask=jnp.asarray(padmask))


def _tile_j(x, sub):
    nsub = x.shape[0] // sub
    y = jnp.broadcast_to(x.reshape(nsub, 1, sub, x.shape[1]), (nsub, sub, sub, x.shape[1]))
    return y.reshape(nsub * sub * sub, x.shape[1])


def _hgrn_front(p, lb, padmask, tb, sub):
    fc, qc, ic = p[:, C_FC:C_FC + HC], p[:, C_QC:C_QC + HC], p[:, C_IC:C_IC + HC]
    f = lb + (1.0 - lb) * _sigmoid(fc)
    kk = 1.0 - f
    g = jnp.log(f)
    if padmask is not None:
        kk = kk * padmask
        g = g * padmask
    qq = _silu(qc)
    b = _mm_hi(tb["tril"][...], g)
    bi = _mm_hi(tb["repi"][...], b)
    bj = _tile_j(b, sub)
    w = jnp.exp(jnp.minimum(bi - bj, 0.0))
    qi = jnp.dot(tb["repim"][...], qq.astype(BF16), preferred_element_type=F32)
    wm = qi * _tile_j(kk, sub) * w
    ab = jnp.dot(wm.astype(BF16), tb["onesbd"][...], preferred_element_type=F32)
    cm = ab * _tile_j(ic, sub)
    o_diag = jnp.dot(tb["sel"][...], cm.astype(BF16), preferred_element_type=F32)
    blrep = _mm_hi(tb["lastsel"][...], b)
    return o_diag, qq * jnp.exp(b), kk * jnp.exp(blrep - b), jnp.exp(blrep)


def _hgrn_finish(o, p, gain, bdf):
    ms = _mm_hi(o * o, bdf) * (1.0 / DV_C)
    return o * lax.rsqrt(ms + EPS) * gain * _silu(p[:, C_GC:C_GC + HC])


def _group_norm_gate(o, gate):
    c = o - jnp.mean(o, axis=-1, keepdims=True)
    return c * lax.rsqrt(jnp.mean(c * c, axis=-1, keepdims=True) + EPS) * _silu(gate)


def _softmax_sink_pv(s, sinkcol, v):
    m = jnp.maximum(jnp.max(s, axis=-1, keepdims=True), sinkcol)
    e = jnp.exp(s - m)
    den = jnp.sum(e, axis=-1, keepdims=True) + jnp.exp(sinkcol - m)
    return _mm(e, v) / den


def _col_mask(width, sub, idx):
    return (lax.broadcasted_iota(I32, (1, width), 1) // sub == idx).astype(F32)


def _mixer_prompt_kernel(scal_ref, p_ref, prev_ref, bias_ref, dmat_ref, qdec_ref, kdec_ref, lb_ref, gain_ref,
                         tril_ref, lastsel_ref, repi_ref, repim_ref, sel_ref, onesbd_ref, bdf_ref,
                         mix_ref, rets_ref, hgs_ref, sret_ref, shg_ref):
    t = pl.program_id(1)

    @pl.when(t == 0)
    def _():
        sret_ref[...] = jnp.zeros_like(sret_ref)
        shg_ref[...] = jnp.zeros_like(shg_ref)

    p = p_ref[...]
    prev = prev_ref[...]
    half = lax.broadcasted_iota(I32, (GQ_A * BLK, 1), 0) < BLK

    for kh in range(KVH_A):
        q = jnp.concatenate([p[:, C_QA + (kh * GQ_A + g) * HD_A:C_QA + (kh * GQ_A + g + 1) * HD_A]
                             for g in range(GQ_A)], axis=0)
        kband = jnp.concatenate([prev[:, kh * HD_A:(kh + 1) * HD_A],
                                 p[:, C_KA + kh * HD_A:C_KA + (kh + 1) * HD_A]], axis=0)
        vband = jnp.concatenate([prev[:, KVH_A * HD_A + kh * HD_A:KVH_A * HD_A + (kh + 1) * HD_A],
                                 p[:, C_VA + kh * HD_A:C_VA + (kh + 1) * HD_A]], axis=0)
        s = _mm_nt(q, kband) * (HD_A ** -0.5) + bias_ref[0, kh]
        sinkcol = jnp.where(half, scal_ref[kh * GQ_A], scal_ref[kh * GQ_A + 1])
        o = _softmax_sink_pv(s, sinkcol, vband)
        for g in range(GQ_A):
            c0 = (kh * GQ_A + g) * HD_A
            mix_ref[:, c0:c0 + HD_A] = o[g * BLK:(g + 1) * BLK]

    kd_t = (p[:, C_KB:C_KB + H_B * DK_B] * kdec_ref[...]).T
    qd = p[:, C_QB:C_QB + H_B * DK_B] * qdec_ref[...]
    for h in range(H_B):
        q = p[:, C_QB + h * DK_B:C_QB + (h + 1) * DK_B]
        k = p[:, C_KB + h * DK_B:C_KB + (h + 1) * DK_B] * (DK_B ** -0.5)
        v = p[:, C_VB + h * DV_B:C_VB + (h + 1) * DV_B]
        a = _mm_nt(q, k) * dmat_ref[h]
        s0 = sret_ref[h]
        o = _mm(a, v) + _mm(qd[:, h * DK_B:(h + 1) * DK_B], s0)
        sret_ref[h] = s0 * scal_ref[H_A + h] + _mm(kd_t[h * DK_B:(h + 1) * DK_B], v)
        c0 = H_A * HD_A + h * DV_B
        mix_ref[:, c0:c0 + DV_B] = _group_norm_gate(o, p[:, C_GB + h * DV_B:C_GB + (h + 1) * DV_B])

    tb = dict(tril=tril_ref, lastsel=lastsel_ref, repi=repi_ref, repim=repim_ref, sel=sel_ref,
              onesbd=onesbd_ref)
    o_diag, qe, k2, ebl = _hgrn_front(p, lb_ref[...], None, tb, SUB_P)
    k2_t, ebl_t = k2.T, ebl.T
    v = p[:, C_IC:C_IC + HC]
    bdf = bdf_ref[...]
    st = shg_ref[...]
    o_inter = []
    for i in range(BLK // SUB_P):
        o_inter.append(_mm(qe[i * SUB_P:(i + 1) * SUB_P], st))
        u = _mm(k2_t * _col_mask(BLK, SUB_P, i), v) * bdf
        st = ebl_t[:, i * SUB_P:i * SUB_P + 1] * st + u
    shg_ref[...] = st
    o = jnp.concatenate(o_inter, axis=0) + o_diag
    c0 = H_A * HD_A + H_B * DV_B
    mix_ref[:, c0:c0 + HC] = _hgrn_finish(o, p, gain_ref[...], bdf)

    @pl.when(t == pl.num_programs(1) - 1)
    def _():
        rets_ref[0] = sret_ref[...]
        hgs_ref[0] = shg_ref[...]


def _mixer_prompt(proj, scal, lb, gain, bp, t_len, tabs, htab):
    nb = t_len // BLK
    const2 = lambda b, t: (0, 0)
    const3 = lambda b, t: (0, 0, 0)
    full = lambda a: pl.BlockSpec(a.shape, const2 if a.ndim == 2 else const3)
    consts = [tabs["dmat"], tabs["qdec"], tabs["kdec"], lb, gain,
              htab["tril"], htab["lastsel"], htab["repi"], htab["repim"], htab["sel"], htab["onesbd"], htab["bdf"]]
    return pl.pallas_call(
        _mixer_prompt_kernel,
        grid=(bp, nb),
        in_specs=[pl.BlockSpec(memory_space=pltpu.SMEM),
                  pl.BlockSpec((BLK, D_PROJ), lambda b, t: (b * nb + t, 0)),
                  pl.BlockSpec((BLK, 2 * KVH_A * HD_A), lambda b, t: (b * nb + jnp.maximum(t - 1, 0), 1)),
                  pl.BlockSpec((1, KVH_A, GQ_A * BLK, 2 * BLK), lambda b, t: (jnp.minimum(t, 1), 0, 0, 0))]
                 + [full(a) for a in consts],
        out_specs=[pl.BlockSpec((BLK, D_MODEL), lambda b, t: (b * nb + t, 0)),
                   pl.BlockSpec((1, H_B, DK_B, DV_B), lambda b, t: (b, 0, 0, 0)),
                   pl.BlockSpec((1, HC, HC), lambda b, t: (b, 0, 0))],
        out_shape=[jax.ShapeDtypeStruct((bp * t_len, D_MODEL), F32),
                   jax.ShapeDtypeStruct((bp, H_B, DK_B, DV_B), F32),
                   jax.ShapeDtypeStruct((bp, HC, HC), F32)],
        scratch_shapes=[pltpu.VMEM((H_B, DK_B, DV_B), F32), pltpu.VMEM((HC, HC), F32)],
        compiler_params=_cparams("parallel", "arbitrary"),
        name="mixer_prompt",
    )(scal, proj, proj, tabs["bias"], *consts)


def _mixer_sample_kernel(scal_ref, p_ref, kbuf_ref, vbuf_ref, rs_ref, hs_ref, bias_ref, dblk_ref, qdec_ref,
                         kdec_ref, lb_ref, gain_ref, padmask_ref,
                         tril_ref, lastsel_ref, repi_ref, repim_ref, sel_ref, onesbd_ref, bdf_ref,
                         mix_ref, rso_ref, hso_ref, kall_ref, vall_ref, oint_ref):
    p = p_ref[...]
    half = lax.broadcasted_iota(I32, (GQ_A * T_PAD, 1), 0) < T_PAD
    kall_ref[...] = jnp.zeros_like(kall_ref)
    vall_ref[...] = jnp.zeros_like(vall_ref)

    for bb in range(GB):
        r0 = bb * T_PAD
        kall_ref[0:WINDOW] = kbuf_ref[bb]
        vall_ref[0:WINDOW] = vbuf_ref[bb]
        kall_ref[WINDOW:WINDOW + T_PAD] = p[r0:r0 + T_PAD, C_KA:C_KA + KVH_A * HD_A]
        vall_ref[WINDOW:WINDOW + T_PAD] = p[r0:r0 + T_PAD, C_VA:C_VA + KVH_A * HD_A]
        kall, vall = kall_ref[...], vall_ref[...]
        for kh in range(KVH_A):
            q = jnp.concatenate([p[r0:r0 + T_PAD, C_QA + (kh * GQ_A + g) * HD_A:C_QA + (kh * GQ_A + g + 1) * HD_A]
                                 for g in range(GQ_A)], axis=0)
            s = _mm_nt(q, kall[:, kh * HD_A:(kh + 1) * HD_A]) * (HD_A ** -0.5) + bias_ref[kh]
            sinkcol = jnp.where(half, scal_ref[kh * GQ_A], scal_ref[kh * GQ_A + 1])
            o = _softmax_sink_pv(s, sinkcol, vall[:, kh * HD_A:(kh + 1) * HD_A])
            for g in range(GQ_A):
                c0 = (kh * GQ_A + g) * HD_A
                mix_ref[r0:r0 + T_PAD, c0:c0 + HD_A] = o[g * T_PAD:(g + 1) * T_PAD]

    kd_t = (p[:, C_KB:C_KB + H_B * DK_B] * kdec_ref[...]).T
    qd = p[:, C_QB:C_QB + H_B * DK_B] * qdec_ref[...]
    for h in range(H_B):
        q = p[:, C_QB + h * DK_B:C_QB + (h + 1) * DK_B]
        k = p[:, C_KB + h * DK_B:C_KB + (h + 1) * DK_B] * (DK_B ** -0.5)
        v = p[:, C_VB + h * DV_B:C_VB + (h + 1) * DV_B]
        a = _mm_nt(q, k) * dblk_ref[h]
        o_intra = _mm(a, v)
        for bb in range(GB):
            r0 = bb * T_PAD
            s0 = rs_ref[bb, h]
            oint_ref[r0:r0 + T_PAD, 0:DV_B] = _mm(qd[r0:r0 + T_PAD, h * DK_B:(h + 1) * DK_B], s0)
            rso_ref[bb, h] = (s0 * scal_ref[H_A + h]
                              + _mm(kd_t[h * DK_B:(h + 1) * DK_B] * _col_mask(BLK, T_PAD, bb), v))
        o = o_intra + oint_ref[:, 0:DV_B]
        c0 = H_A * HD_A + h * DV_B
        mix_ref[:, c0:c0 + DV_B] = _group_norm_gate(o, p[:, C_GB + h * DV_B:C_GB + (h + 1) * DV_B])

    tb = dict(tril=tril_ref, lastsel=lastsel_ref, repi=repi_ref, repim=repim_ref, sel=sel_ref,
              onesbd=onesbd_ref)
    o_diag, qe, k2, ebl = _hgrn_front(p, lb_ref[...], padmask_ref[...], tb, T_PAD)
    k2_t, ebl_t = k2.T, ebl.T
    v = p[:, C_IC:C_IC + HC]
    for bb in range(GB):
        r0 = bb * T_PAD
        k2m = k2_t * _col_mask(BLK, T_PAD, bb)
        for h in range(H_C):
            s0 = hs_ref[bb, h]
            oint_ref[r0:r0 + T_PAD, h * DV_C:(h + 1) * DV_C] = _mm(qe[r0:r0 + T_PAD, h * DK_C:(h + 1) * DK_C], s0)
            hso_ref[bb, h] = (ebl_t[h * DK_C:(h + 1) * DK_C, r0:r0 + 1] * s0
                              + _mm(k2m[h * DK_C:(h + 1) * DK_C], v[:, h * DV_C:(h + 1) * DV_C]))
    o = oint_ref[...] + o_diag
    c0 = H_A * HD_A + H_B * DV_B
    mix_ref[:, c0:c0 + HC] = _hgrn_finish(o, p, gain_ref[...], bdf_ref[...])


def _mixer_sample(proj8, scal, kbuf, vbuf, rstate, hstate, lb, gain, tabs, htab):
    bs = kbuf.shape[0]
    full = lambda a: pl.BlockSpec(a.shape, (lambda i: (0, 0)) if a.ndim == 2 else (lambda i: (0, 0, 0)))
    consts = [tabs["bias"], tabs["dblk"], tabs["qdec"], tabs["kdec"], lb, gain, tabs["padmask"],
              htab["tril"], htab["lastsel"], htab["repi"], htab["repim"], htab["sel"], htab["onesbd"], htab["bdf"]]
    return pl.pallas_call(
        _mixer_sample_kernel,
        grid=(bs // GB,),
        in_specs=[pl.BlockSpec(memory_space=pltpu.SMEM),
                  pl.BlockSpec((BLK, D_PROJ), lambda i: (i, 0)),
                  pl.BlockSpec((GB, WINDOW, KVH_A * HD_A), lambda i: (i, 0, 0)),
                  pl.BlockSpec((GB, WINDOW, KVH_A * HD_A), lambda i: (i, 0, 0)),
                  pl.BlockSpec((GB, H_B, DK_B, DV_B), lambda i: (i, 0, 0, 0)),
                  pl.BlockSpec((GB, H_C, DK_C, DV_C), lambda i: (i, 0, 0, 0))]
                 + [full(a) for a in consts],
        out_specs=[pl.BlockSpec((BLK, D_MODEL), lambda i: (i, 0)),
                   pl.BlockSpec((GB, H_B, DK_B, DV_B), lambda i: (i, 0, 0, 0)),
                   pl.BlockSpec((GB, H_C, DK_C, DV_C), lambda i: (i, 0, 0, 0))],
        out_shape=[jax.ShapeDtypeStruct((bs * T_PAD, D_MODEL), F32),
                   jax.ShapeDtypeStruct(rstate.shape, F32),
                   jax.ShapeDtypeStruct(hstate.shape, F32)],
        scratch_shapes=[pltpu.VMEM((2 * WINDOW, KVH_A * HD_A), F32), pltpu.VMEM((2 * WINDOW, KVH_A * HD_A), F32),
                        pltpu.VMEM((BLK, HC), F32)],
        compiler_params=_cparams("parallel"),
        name="mixer_sample",
    )(scal, proj8, kbuf, vbuf, rstate, hstate, *consts)


def _attend_memory(q, mk, mv):
    outs = []
    for h in range(XH):
        sl = slice(h * XHD, (h + 1) * XHD)
        s = _mm_nt(q[:, sl], mk[:, sl]) * (XHD ** -0.5)
        e = jnp.exp(s - jnp.max(s, axis=-1, keepdims=True))
        outs.append(_mm(e, mv[:, sl]) / jnp.sum(e, axis=-1, keepdims=True))
    return jnp.concatenate(outs, axis=-1)


def _xattn_prompt_kernel(x_ref, g_ref, wq_ref, wo_ref, mk_ref, mv_ref, o_ref, wq_bf, wo_bf):
    @pl.when((pl.program_id(0) == 0) & (pl.program_id(1) == 0))
    def _():
        wq_bf[...] = wq_ref[...].astype(BF16)
        wo_bf[...] = wo_ref[...].astype(BF16)

    x = x_ref[...]
    q = jnp.dot(_rms(x, g_ref[...]).astype(BF16), wq_bf[...], preferred_element_type=F32)
    o = _attend_memory(q, mk_ref[...], mv_ref[...])
    o_ref[...] = x + jnp.dot(o.astype(BF16), wo_bf[...], preferred_element_type=F32)


def _xattn_prompt(x, g, wq, wo, mk, mv, bp, t_len):
    tq = _pick(t_len, (512, 256, 128))
    nt = t_len // tq
    dq = XH * XHD
    return pl.pallas_call(
        _xattn_prompt_kernel,
        grid=(bp, nt),
        in_specs=[pl.BlockSpec((tq, D_MODEL), lambda b, t: (b * nt + t, 0)),
                  pl.BlockSpec((1, D_MODEL), lambda b, t: (0, 0)),
                  pl.BlockSpec((D_MODEL, dq), lambda b, t: (0, 0)),
                  pl.BlockSpec((dq, D_MODEL), lambda b, t: (0, 0)),
                  pl.BlockSpec((N_MEM, dq), lambda b, t: (b, 0)),
                  pl.BlockSpec((N_MEM, dq), lambda b, t: (b, 0))],
        out_specs=pl.BlockSpec((tq, D_MODEL), lambda b, t: (b * nt + t, 0)),
        out_shape=jax.ShapeDtypeStruct(x.shape, F32),
        scratch_shapes=[pltpu.VMEM((D_MODEL, dq), BF16), pltpu.VMEM((dq, D_MODEL), BF16)],
        input_output_aliases={0: 0},
        compiler_params=_cparams("arbitrary", "arbitrary"),
        name="xattn_prompt",
    )(x, g.reshape(1, D_MODEL), wq, wo, mk, mv)


XB = 8


def _xattn_sample_kernel(q_ref, mk_ref, mv_ref, o_ref):
    for bb in range(XB):
        r0 = bb * T_PAD
        o_ref[r0:r0 + T_PAD] = _attend_memory(q_ref[r0:r0 + T_PAD], mk_ref[bb], mv_ref[bb])


def _xattn_sample(q8, mk, mv):
    bs = mk.shape[0]
    dq = XH * XHD
    return pl.pallas_call(
        _xattn_sample_kernel,
        grid=(bs // XB,),
        in_specs=[pl.BlockSpec((XB * T_PAD, dq), lambda i: (i, 0)),
                  pl.BlockSpec((XB, N_MEM, dq), lambda i: (i, 0, 0)),
                  pl.BlockSpec((XB, N_MEM, dq), lambda i: (i, 0, 0))],
        out_specs=pl.BlockSpec((XB * T_PAD, dq), lambda i: (i, 0)),
        out_shape=jax.ShapeDtypeStruct((bs * T_PAD, dq), F32),
        compiler_params=_cparams("parallel"),
        name="xattn_sample",
    )(q8, mk, mv)


def _ffn_kernel(be_ref, nused_ref, x_ref, g_ref, w1_ref, w3_ref, w2_ref, o_ref, xn_ref, *, dense):
    i, j = pl.program_id(0), pl.program_id(1)
    used = i < nused_ref[0]

    @pl.when(j == 0)
    def _():
        x = x_ref[...]
        if dense:
            xn_ref[...] = _rms(x, g_ref[...]).astype(BF16)
            o_ref[...] = x
        else:
            xn_ref[...] = x.astype(BF16)
            o_ref[...] = jnp.zeros_like(o_ref)

    @pl.when(used)
    def _():
        xn = xn_ref[...]
        h1 = jnp.dot(xn, w1_ref[0].astype(BF16), preferred_element_type=F32)
        h3 = jnp.dot(xn, w3_ref[0].astype(BF16), preferred_element_type=F32)
        o_ref[...] += _mm(_silu(h1) * h3, w2_ref[0])


def _ffn(x, g, w1, w3, w2, block_e, nused, tm, dense):
    rows = x.shape[0]
    dff = w1.shape[2]
    tf = _pick(dff, (512, 256, 128))
    nb = rows // tm

    def xmap(i, j, be, nu):
        return (jnp.minimum(i, nu[0] - 1), 0)

    def w13map(i, j, be, nu):
        return (be[jnp.minimum(i, nu[0] - 1)], 0, jnp.where(i < nu[0], j, dff // tf - 1))

    def w2map(i, j, be, nu):
        return (be[jnp.minimum(i, nu[0] - 1)], jnp.where(i < nu[0], j, dff // tf - 1), 0)

    grid_spec = pltpu.PrefetchScalarGridSpec(
        num_scalar_prefetch=2,
        grid=(nb, dff // tf),
        in_specs=[pl.BlockSpec((tm, D_MODEL), xmap),
                  pl.BlockSpec((1, D_MODEL), lambda i, j, be, nu: (0, 0)),
                  pl.BlockSpec((1, D_MODEL, tf), w13map),
                  pl.BlockSpec((1, D_MODEL, tf), w13map),
                  pl.BlockSpec((1, tf, D_MODEL), w2map)],
        out_specs=pl.BlockSpec((tm, D_MODEL), lambda i, j, be, nu: (i, 0)),
        scratch_shapes=[pltpu.VMEM((tm, D_MODEL), BF16)],
    )
    return pl.pallas_call(
        functools.partial(_ffn_kernel, dense=dense),
        grid_spec=grid_spec,
        out_shape=jax.ShapeDtypeStruct((rows, D_MODEL), F32),
        input_output_aliases={2: 0} if dense else {},
        compiler_params=_cparams("arbitrary", "arbitrary"),
        name="ffn_dense" if dense else "ffn_experts",
    )(block_e, nused, x, g.reshape(1, D_MODEL), w1, w3, w2)


LANES = 128


def _router_kernel(x_ref, g_ref, wr_ref, h_ref, idx_ref, gate_ref):
    h = _rms(x_ref[...], g_ref[...])
    h_ref[...] = h
    logits = _mm_hi(h, wr_ref[...])
    lane = lax.broadcasted_iota(I32, logits.shape, 1)
    logits = jnp.where(lane < N_EXPERTS, logits, NEG_INF)
    m1 = jnp.max(logits, axis=-1, keepdims=True)
    i1 = jnp.min(jnp.where(logits == m1, lane, LANES), axis=-1, keepdims=True)
    rest = jnp.where(lane == i1, NEG_INF, logits)
    m2 = jnp.max(rest, axis=-1, keepdims=True)
    i2 = jnp.min(jnp.where(rest == m2, lane, LANES), axis=-1, keepdims=True)
    e2 = jnp.exp(m2 - m1)
    den = 1.0 + e2
    idx_ref[...] = jnp.where(lane == 0, i1, jnp.where(lane == 1, i2, 0))
    gate_ref[...] = jnp.where(lane == 0, 1.0 / den, jnp.where(lane == 1, e2 / den, 0.0))


def _router(x, g, w_router):
    n = x.shape[0]
    tm = _pick(n, (512, 256, 128, 64, 32, 16, 8))
    wr = jnp.pad(w_router, ((0, 0), (0, LANES - N_EXPERTS)))
    return pl.pallas_call(
        _router_kernel,
        grid=(n // tm,),
        in_specs=[pl.BlockSpec((tm, D_MODEL), lambda i: (i, 0)),
                  pl.BlockSpec((1, D_MODEL), lambda i: (0, 0)),
                  pl.BlockSpec((D_MODEL, LANES), lambda i: (0, 0))],
        out_specs=[pl.BlockSpec((tm, D_MODEL), lambda i: (i, 0)),
                   pl.BlockSpec((tm, LANES), lambda i: (i, 0)),
                   pl.BlockSpec((tm, LANES), lambda i: (i, 0))],
        out_shape=[jax.ShapeDtypeStruct((n, D_MODEL), F32),
                   jax.ShapeDtypeStruct((n, LANES), I32),
                   jax.ShapeDtypeStruct((n, LANES), F32)],
        compiler_params=_cparams("parallel"),
        name="router",
    )(x, g.reshape(1, D_MODEL), wr)


def _gather_rows_kernel(tok_ref, src_ref, o_ref, sem, *, tg):
    base = pl.program_id(0) * tg

    def row_copy(r, tok):
        return pltpu.make_async_copy(src_ref.at[pl.ds(tok, 1)], o_ref.at[pl.ds(r, 1)], sem)

    def issue(r, c):
        row_copy(r, tok_ref[base + r]).start()
        return c

    lax.fori_loop(0, tg, issue, 0)

    def drain(r, c):
        row_copy(r, 0).wait()
        return c

    lax.fori_loop(0, tg, drain, 0)


def _gather_rows(src, tok, tg):
    rows = tok.shape[0]
    grid_spec = pltpu.PrefetchScalarGridSpec(
        num_scalar_prefetch=1,
        grid=(rows // tg,),
        in_specs=[pl.BlockSpec(memory_space=pl.ANY)],
        out_specs=pl.BlockSpec((tg, D_MODEL), lambda i, tok: (i, 0)),
        scratch_shapes=[pltpu.SemaphoreType.DMA],
    )
    return pl.pallas_call(
        functools.partial(_gather_rows_kernel, tg=tg),
        grid_spec=grid_spec,
        out_shape=jax.ShapeDtypeStruct((rows, D_MODEL), F32),
        compiler_params=_cparams("arbitrary"),
        name="moe_gather",
    )(tok, src)


def _combine_kernel(pos_ref, x_ref, gate_ref, y_ref, o_ref, buf_ref, sem, *, tc):
    base = pl.program_id(0) * tc

    def row_copy(k, r, pos):
        return pltpu.make_async_copy(y_ref.at[pl.ds(pos, 1)], buf_ref.at[k, pl.ds(r, 1)], sem)

    def issue(r, c):
        for k in range(TOP_K):
            row_copy(k, r, pos_ref[(base + r) * TOP_K + k]).start()
        return c

    lax.fori_loop(0, tc, issue, 0)

    def drain(r, c):
        for k in range(TOP_K):
            row_copy(k, r, 0).wait()
        return c

    lax.fori_loop(0, tc, drain, 0)
    gate = gate_ref[...]
    o_ref[...] = x_ref[...] + (buf_ref[0] * gate[:, 0:1] + buf_ref[1] * gate[:, 1:2])


def _combine(x, gate, y, pos, tc):
    n = x.shape[0]
    grid_spec = pltpu.PrefetchScalarGridSpec(
        num_scalar_prefetch=1,
        grid=(n // tc,),
        in_specs=[pl.BlockSpec((tc, D_MODEL), lambda i, pos: (i, 0)),
                  pl.BlockSpec((tc, LANES), lambda i, pos: (i, 0)),
                  pl.BlockSpec(memory_space=pl.ANY)],
        out_specs=pl.BlockSpec((tc, D_MODEL), lambda i, pos: (i, 0)),
        scratch_shapes=[pltpu.VMEM((TOP_K, tc, D_MODEL), F32), pltpu.SemaphoreType.DMA],
    )
    return pl.pallas_call(
        functools.partial(_combine_kernel, tc=tc),
        grid_spec=grid_spec,
        out_shape=jax.ShapeDtypeStruct(x.shape, F32),
        input_output_aliases={1: 0},
        compiler_params=_cparams("arbitrary"),
        name="moe_combine",
    )(pos, x, gate, y)


def _moe(x, g, w_router, w1, w3, w2):
    n = x.shape[0]
    tm = _pick(n, (512, 256, 128, 64, 32, 16, 8))
    h, idx, gate = _router(x, g, w_router)
    a = n * TOP_K
    e_flat = idx[:, :TOP_K].reshape(a)
    onehot = (e_flat[:, None] == jnp.arange(N_EXPERTS, dtype=I32)[None, :]).astype(I32)
    csum = jnp.cumsum(onehot, axis=0)
    counts = csum[-1]
    rank = jnp.sum((csum - onehot) * onehot, axis=1)
    padded = (counts + tm - 1) // tm * tm
    pend = jnp.cumsum(padded)
    pstart = pend - padded
    dest = (pstart[e_flat] + rank).astype(I32)
    nb = a // tm + N_EXPERTS
    tok = jnp.zeros((nb * tm,), I32).at[dest].set(jnp.repeat(jnp.arange(n, dtype=I32), TOP_K))
    block_e = jnp.minimum(jnp.searchsorted(pend, jnp.arange(nb, dtype=I32) * tm, side="right"),
                          N_EXPERTS - 1).astype(I32)
    nused = (pend[-1:] // tm).astype(I32)
    xg = _gather_rows(h, tok, tm)
    y = _ffn(xg, g, w1, w3, w2, block_e, nused, tm, dense=False)
    return _combine(x, gate, y, dest, _pick(n, (256, 128, 64, 32, 16, 8)))


def kernel(x_prompt, x_sample, cache_swa_k, cache_swa_v, state_ret, state_hgrn, cache_mem_k, cache_mem_v, mem_prompt, norm_mix, w_in, swa_sinks, hgrn_lb_logits, hgrn_norm, w_out, norm_xattn, norm_mem, wx_q, wx_k, wx_v, wx_o, norm_ffn, ffn_w1, ffn_w3, ffn_w2, moe_router, moe_w1, moe_w3, moe_w2, final_norm):
    bp, t_len, _ = x_prompt.shape
    bs, ts, _ = x_sample.shape
    depth = w_in.shape[0]
    n_p, n_s = bp * t_len, bs * ts
    n = n_p + n_s
    assert t_len % BLK == 0 and bs % GB == 0 and bs % XB == 0 and ts <= T_PAD
    assert cache_swa_k.shape[2] == WINDOW and ts < WINDOW
    dq = XH * XHD

    x = jnp.concatenate([x_prompt.reshape(n_p, D_MODEL), x_sample.reshape(n_s, D_MODEL)], axis=0)
    ptab, stab = _prompt_tables(), _sample_tables(ts)
    htab_p, htab_s = _hgrn_tables(SUB_P), _hgrn_tables(T_PAD)
    lb_p = jax.nn.softmax(hgrn_lb_logits.astype(F32), axis=0)
    lower_bounds = jnp.cumsum(lb_p, axis=0) - lb_p[0]
    mem = mem_prompt.reshape(bp * N_MEM, D_MODEL)

    def pad_tokens(a):
        return jnp.pad(a.reshape(bs, ts, -1), ((0, 0), (0, T_PAD - ts), (0, 0))).reshape(bs * T_PAD, -1)

    def unpad_tokens(a):
        return a.reshape(bs, T_PAD, -1)[:, :ts].reshape(bs * ts, -1)

    outs = {k: [] for k in ("pk", "pv", "pr", "pc", "pmk", "pmv", "sk", "sv", "sr", "sc")}
    for l in range(depth):
        proj = _rms_matmul(x, norm_mix[l], w_in[l])
        lb = lower_bounds[l].reshape(1, HC)
        gain = hgrn_norm[l].reshape(1, HC)
        scal_p = jnp.concatenate([swa_sinks[l].astype(F32), jnp.asarray(ptab["cdec"], F32)])
        scal_s = jnp.concatenate([swa_sinks[l].astype(F32), jnp.asarray(stab["cdec"], F32)])
        mix_p, ret_p, hg_p = _mixer_prompt(proj, scal_p, lb, gain, bp, t_len, ptab, htab_p)
        proj_s = proj[n_p:]
        mix_s, ret_s, hg_s = _mixer_sample(
            pad_tokens(proj_s), scal_s,
            cache_swa_k[l].reshape(bs, WINDOW, KVH_A * HD_A), cache_swa_v[l].reshape(bs, WINDOW, KVH_A * HD_A),
            state_ret[l], state_hgrn[l], lb, gain, stab, htab_s)
        x = _matmul_res(mix_p, w_out[l], x, 0)
        x = _matmul_res(unpad_tokens(mix_s), w_out[l], x, n_p)

        proj_p = proj[:n_p].reshape(bp, t_len, D_PROJ)
        outs["pk"].append(proj_p[:, t_len - WINDOW:, C_KA:C_KA + KVH_A * HD_A].reshape(bp, WINDOW, KVH_A, HD_A))
        outs["pv"].append(proj_p[:, t_len - WINDOW:, C_VA:C_VA + KVH_A * HD_A].reshape(bp, WINDOW, KVH_A, HD_A))
        outs["pr"].append(ret_p)
        hg_diag = jnp.stack([hg_p[:, h * DK_C:(h + 1) * DK_C, h * DV_C:(h + 1) * DV_C] for h in range(H_C)], axis=1)
        outs["pc"].append(hg_diag)
        new_k = proj_s[:, C_KA:C_KA + KVH_A * HD_A].reshape(bs, ts, KVH_A, HD_A)
        new_v = proj_s[:, C_VA:C_VA + KVH_A * HD_A].reshape(bs, ts, KVH_A, HD_A)
        outs["sk"].append(jnp.concatenate([cache_swa_k[l][:, ts:].astype(F32), new_k], axis=1))
        outs["sv"].append(jnp.concatenate([cache_swa_v[l][:, ts:].astype(F32), new_v], axis=1))
        outs["sr"].append(ret_s)
        outs["sc"].append(hg_s)

        mk = _rms_matmul(mem, norm_mem[l], wx_k[l])
        mv = _rms_matmul(mem, norm_mem[l], wx_v[l])
        outs["pmk"].append(mk.reshape(bp, N_MEM, XH, XHD))
        outs["pmv"].append(mv.reshape(bp, N_MEM, XH, XHD))
        x = _xattn_prompt(x, norm_xattn[l], wx_q[l], wx_o[l], mk, mv, bp, t_len)
        q_s = _rms_matmul(x, norm_xattn[l], wx_q[l], row_off=n_p, rows=n_s)
        o_s = _xattn_sample(pad_tokens(q_s), cache_mem_k[l].reshape(bs, N_MEM, dq).astype(F32),
                            cache_mem_v[l].reshape(bs, N_MEM, dq).astype(F32))
        x = _matmul_res(unpad_tokens(o_s), wx_o[l], x, n_p)

        e = l // 2
        if l % 2 == 0:
            tm = _pick(n, (1536, 768, 512, 256, 128, 64, 32, 16, 8))
            nb = n // tm
            x = _ffn(x, norm_ffn[l], ffn_w1[e][None], ffn_w3[e][None], ffn_w2[e][None],
                     jnp.zeros((nb,), I32), jnp.full((1,), nb, I32), tm, dense=True)
        else:
            x = _moe(x, norm_ffn[l], moe_router[e], moe_w1[e], moe_w3[e], moe_w2[e])

    y_p = _rmsnorm_rows(x, final_norm, 0, n_p).reshape(bp, t_len, D_MODEL)
    y_s = _rmsnorm_rows(x, final_norm, n_p, n_s).reshape(bs, ts, D_MODEL)
    st = lambda k: jnp.stack(outs[k])
    return (y_p, y_s, st("pk"), st("pv"), st("pr"), st("pc"), st("pmk"), st("pmv"),
            st("sk"), st("sv"), st("sr"), st("sc"))
```

```python
import functools

import numpy as np
import jax
import jax.numpy as jnp
from jax import lax
from jax.experimental import pallas as pl
from jax.experimental.pallas import tpu as pltpu

F32, BF16, I32 = jnp.float32, jnp.bfloat16, jnp.int32
HIGHEST = lax.Precision.HIGHEST
NEG_INF = float("-inf")

D_MODEL = 1024
WINDOW = 128
H_A, KVH_A, GQ_A, HD_A = 4, 2, 2, 64
ALIBI_MAX_EXP = 8.0
H_B, DK_B, DV_B = 4, 64, 128
RET_CHUNK = 128
H_C, DK_C, DV_C = 4, 64, 64
N_MEM, XH, XHD = 256, 4, 128
N_EXPERTS, TOP_K = 8, 2
EPS = 1e-6
D_PROJ = 3072
C_QA, C_KA, C_VA = 0, 256, 384
C_QB, C_KB, C_VB, C_GB = 512, 768, 1024, 1536
C_FC, C_QC, C_IC, C_GC = 2048, 2304, 2560, 2816
KV_A = KVH_A * HD_A
HC = H_C * DK_C

BLK = 128
SUB_P = 16
T_PAD = 8
GB = BLK // T_PAD
LANES = 128
VMEM_LIMIT = 56 * 1024 * 1024
ROW_TILES = (512, 256, 128, 64, 32, 16, 8)


def _pick(n, cands):
    for c in cands:
        if n % c == 0:
            return c
    raise ValueError(f"no tile for {n}")


def _row_tile(rows, row_off):
    return _pick(int(np.gcd(rows, row_off)) if row_off else rows, ROW_TILES)


def _cparams(*sem):
    return pltpu.CompilerParams(dimension_semantics=sem, vmem_limit_bytes=VMEM_LIMIT)


def _mm(a, b):
    return jnp.dot(a.astype(BF16), b.astype(BF16), preferred_element_type=F32)


def _mm_nt(a, b):
    return lax.dot_general(a.astype(BF16), b.astype(BF16), (((1,), (1,)), ((), ())),
                           preferred_element_type=F32)


def _mm_hi(a, b):
    return jnp.dot(a, b, precision=HIGHEST, preferred_element_type=F32)


def _sigmoid(x):
    return 1.0 / (1.0 + jnp.exp(-x))


def _silu(x):
    return x * _sigmoid(x)


def _rms(x, g):
    return x * lax.rsqrt(jnp.mean(x * x, axis=-1, keepdims=True) + EPS) * g


def _rms_matmul_kernel(x_ref, g_ref, w_ref, o_ref, xn_ref):
    @pl.when(pl.program_id(1) == 0)
    def _():
        xn_ref[...] = _rms(x_ref[...], g_ref[0]).astype(BF16)

    o_ref[...] = jnp.dot(xn_ref[...], w_ref[0], preferred_element_type=F32)


def _rms_matmul(x, g, w, l, row_off=0, rows=None):
    n, k = x.shape
    m = w.shape[2]
    rows = n if rows is None else rows
    tm = _row_tile(rows, row_off)
    tn = _pick(m, (1536, 1024, 512, 256, 128))
    off = row_off // tm
    return pl.pallas_call(
        _rms_matmul_kernel,
        grid=(rows // tm, m // tn),
        in_specs=[pl.BlockSpec((tm, k), lambda i, j: (i + off, 0)),
                  pl.BlockSpec((1, 1, k), lambda i, j: (l, 0, 0)),
                  pl.BlockSpec((1, k, tn), lambda i, j: (l, 0, j))],
        out_specs=pl.BlockSpec((tm, tn), lambda i, j: (i, j)),
        out_shape=jax.ShapeDtypeStruct((rows, m), F32),
        scratch_shapes=[pltpu.VMEM((tm, k), BF16)],
        compiler_params=_cparams("parallel", "arbitrary"),
        name="rms_matmul",
    )(x, g, w)


def _matmul_res_kernel(a_ref, w_ref, x_ref, o_ref):
    o_ref[...] = x_ref[...] + jnp.dot(a_ref[...].astype(BF16), w_ref[0], preferred_element_type=F32)


def _matmul_res(a, w, l, x, row_off):
    rows, k = a.shape
    m = w.shape[2]
    tm = _row_tile(rows, row_off)
    off = row_off // tm
    return pl.pallas_call(
        _matmul_res_kernel,
        grid=(rows // tm,),
        in_specs=[pl.BlockSpec((tm, k), lambda i: (i, 0)),
                  pl.BlockSpec((1, k, m), lambda i: (l, 0, 0)),
                  pl.BlockSpec((tm, m), lambda i: (i + off, 0))],
        out_specs=pl.BlockSpec((tm, m), lambda i: (i + off, 0)),
        out_shape=jax.ShapeDtypeStruct(x.shape, F32),
        input_output_aliases={2: 0},
        compiler_params=_cparams("parallel"),
        name="matmul_res",
    )(a, w, x)


def _rmsnorm_kernel(x_ref, g_ref, o_ref):
    o_ref[...] = _rms(x_ref[...], g_ref[...])


def _rmsnorm_rows(x, g, row_off, rows):
    n, k = x.shape
    tm = _row_tile(rows, row_off)
    off = row_off // tm
    return pl.pallas_call(
        _rmsnorm_kernel,
        grid=(rows // tm,),
        in_specs=[pl.BlockSpec((tm, k), lambda i: (i + off, 0)),
                  pl.BlockSpec((1, k), lambda i: (0, 0))],
        out_specs=pl.BlockSpec((tm, k), lambda i: (i, 0)),
        out_shape=jax.ShapeDtypeStruct((rows, k), F32),
        compiler_params=_cparams("parallel"),
        name="final_rmsnorm",
    )(x, g.reshape(1, k))


def _retention_gamma():
    return 1.0 - np.exp2(-5.0 - np.arange(H_B, dtype=np.float64))


def _alibi_slopes():
    return np.exp2(-(ALIBI_MAX_EXP / H_A) * np.arange(1, H_A + 1, dtype=np.float64))


def _head_cols(vals_hr, width):
    return np.repeat(vals_hr.T[:, :, None], width, axis=2).reshape(vals_hr.shape[1], -1)


def _hgrn_tables(sub):
    r = np.arange(BLK)
    same = (r[:, None] // sub) == (r[None, :] // sub)
    tril = (same & (r[None, :] <= r[:, None])).astype(np.float32)
    pr = np.arange(BLK * sub)
    sel = (r[:, None] == (pr // sub)[None, :]).astype(np.float32)
    hh = np.arange(HC) // DK_C
    bd = (hh[:, None] == hh[None, :]).astype(np.float32)
    return dict(tril=jnp.asarray(tril), sel=jnp.asarray(sel, dtype=BF16),
                onesbd=jnp.asarray(bd, dtype=BF16), bdf=jnp.asarray(bd))


def _prompt_tables():
    gam = _retention_gamma()
    i = np.arange(RET_CHUNK, dtype=np.float64)
    diff = i[:, None] - i[None, :]
    dmat = np.where(diff >= 0, gam[:, None, None] ** np.maximum(diff, 0.0), 0.0)
    qdec = _head_cols(gam[:, None] ** (i + 1.0), DK_B)
    kdec = _head_cols(gam[:, None] ** (RET_CHUNK - 1.0 - i), DK_B) * (DK_B ** -0.5)
    cdec = gam ** RET_CHUNK
    slopes = _alibi_slopes().reshape(KVH_A, GQ_A)
    qi = np.tile(np.arange(WINDOW), GQ_A)[:, None]
    qg = np.repeat(np.arange(GQ_A), WINDOW)[:, None]
    j = np.arange(2 * WINDOW)[None, :]
    rel = WINDOW + qi - j
    ok = (rel >= 0) & (rel < WINDOW)
    bias = np.stack([np.where(ok, -slopes[kh][qg] * rel, NEG_INF) for kh in range(KVH_A)])
    bias_first = np.where(j < WINDOW, NEG_INF, bias)
    return dict(dmat=jnp.asarray(dmat, dtype=F32), qdec=jnp.asarray(qdec, dtype=F32),
                kdec=jnp.asarray(kdec, dtype=F32), cdec=cdec,
                bias=jnp.asarray(np.stack([bias_first, bias]), dtype=F32))


def _sample_tables(t):
    gam = _retention_gamma()
    r = np.arange(BLK)
    ti, bi = r % T_PAD, r // T_PAD
    diff = (ti[:, None] - ti[None, :]).astype(np.float64)
    same = bi[:, None] == bi[None, :]
    dblk = np.where(same & (diff >= 0), gam[:, None, None] ** np.maximum(diff, 0.0), 0.0)
    qdec = _head_cols(gam[:, None] ** (ti + 1.0), DK_B)
    kdec = _head_cols(gam[:, None] ** (t - 1.0 - ti), DK_B) * (DK_B ** -0.5)
    cdec = gam ** t
    slopes = _alibi_slopes().reshape(KVH_A, GQ_A)
    qt = np.tile(np.arange(T_PAD), GQ_A)[:, None]
    qg = np.repeat(np.arange(GQ_A), T_PAD)[:, None]
    c = np.arange(2 * WINDOW)[None, :]
    rel = qt + WINDOW - c
    ok = (rel >= 0) & (rel < WINDOW) & (c < WINDOW + t)
    bias = np.stack([np.where(ok, -slopes[kh][qg] * rel, NEG_INF) for kh in range(KVH_A)])
    padmask = np.repeat((ti < t).astype(np.float32)[:, None], HC, axis=1)
    return dict(dblk=jnp.asarray(dblk, dtype=F32), qdec=jnp.asarray(qdec, dtype=F32),
                kdec=jnp.asarray(kdec, dtype=F32), cdec=cdec, bias=jnp.asarray(bias, dtype=F32),
                padmask=jnp.asarray(padmask))


def _tile_j(x, sub):
    nsub = x.shape[0] // sub
    y = jnp.broadcast_to(x.reshape(nsub, 1, sub, x.shape[1]), (nsub, sub, sub, x.shape[1]))
    return y.reshape(nsub * sub * sub, x.shape[1])


def _row_bcast(ref, r, sub):
    return jnp.broadcast_to(ref[r:r + 1, :], (sub, ref.shape[1]))


def _hgrn_front(p, lb, padmask, tril, sel, onesbd, b_ref, q_ref, sub):
    fc, qc, ic = p[:, C_FC:C_FC + HC], p[:, C_QC:C_QC + HC], p[:, C_IC:C_IC + HC]
    f = lb + (1.0 - lb) * _sigmoid(fc)
    kk = 1.0 - f
    g = jnp.log(f)
    if padmask is not None:
        kk = kk * padmask
        g = g * padmask
    qq = _silu(qc)
    b = _mm_hi(tril, g)
    b_ref[...] = b
    q_ref[...] = qq
    jrow = lax.broadcasted_iota(I32, (sub, 1), 0)
    bi = jnp.concatenate([_row_bcast(b_ref, r, sub) for r in range(BLK)], axis=0)
    qi = jnp.concatenate([jnp.where(jrow <= r % sub, _row_bcast(q_ref, r, sub), 0.0) for r in range(BLK)], axis=0)
    w = jnp.exp(jnp.minimum(bi - _tile_j(b, sub), 0.0))
    wm = qi * _tile_j(kk, sub) * w
    ab = jnp.dot(wm.astype(BF16), onesbd, preferred_element_type=F32)
    cm = ab * _tile_j(ic, sub)
    o_diag = jnp.dot(sel, cm.astype(BF16), preferred_element_type=F32)
    blrep = jnp.concatenate([_row_bcast(b_ref, i * sub + sub - 1, sub) for i in range(BLK // sub)], axis=0)
    return o_diag, qq * jnp.exp(b), kk * jnp.exp(blrep - b), jnp.exp(blrep)


def _hgrn_finish(o, p, gain, bdf):
    ms = _mm_hi(o * o, bdf) * (1.0 / DV_C)
    return o * lax.rsqrt(ms + EPS) * gain * _silu(p[:, C_GC:C_GC + HC])


def _group_norm_gate(o, gate):
    c = o - jnp.mean(o, axis=-1, keepdims=True)
    return c * lax.rsqrt(jnp.mean(c * c, axis=-1, keepdims=True) + EPS) * _silu(gate)


def _softmax_sink_pv(s, sinkcol, v):
    m = jnp.maximum(jnp.max(s, axis=-1, keepdims=True), sinkcol)
    e = jnp.exp(s - m)
    den = jnp.sum(e, axis=-1, keepdims=True) + jnp.exp(sinkcol - m)
    return _mm(e, v) / den


def _col_mask(width, sub, idx):
    return (lax.broadcasted_iota(I32, (1, width), 1) // sub == idx).astype(F32)


def _mixer_prompt_kernel(scal_ref, p_ref, prev_ref, bias_ref, dmat_ref, qdec_ref, kdec_ref, lb_ref, gain_ref,
                         tril_ref, sel_ref, onesbd_ref, bdf_ref,
                         mix_ref, kkeep_ref, vkeep_ref, rets_ref, hgs_ref,
                         sret_ref, shg_ref, b_ref, q_ref, *, l):
    t = pl.program_id(1)

    @pl.when(t == 0)
    def _():
        sret_ref[...] = jnp.zeros_like(sret_ref)
        shg_ref[...] = jnp.zeros_like(shg_ref)

    p = p_ref[...]
    prev = prev_ref[...]
    half = lax.broadcasted_iota(I32, (GQ_A * BLK, 1), 0) < BLK

    for kh in range(KVH_A):
        q = jnp.concatenate([p[:, C_QA + (kh * GQ_A + g) * HD_A:C_QA + (kh * GQ_A + g + 1) * HD_A]
                             for g in range(GQ_A)], axis=0)
        kband = jnp.concatenate([prev[:, kh * HD_A:(kh + 1) * HD_A],
                                 p[:, C_KA + kh * HD_A:C_KA + (kh + 1) * HD_A]], axis=0)
        vband = jnp.concatenate([prev[:, KV_A + kh * HD_A:KV_A + (kh + 1) * HD_A],
                                 p[:, C_VA + kh * HD_A:C_VA + (kh + 1) * HD_A]], axis=0)
        s = _mm_nt(q, kband) * (HD_A ** -0.5) + bias_ref[0, kh]
        sinkcol = jnp.where(half, scal_ref[l, kh * GQ_A], scal_ref[l, kh * GQ_A + 1])
        o = _softmax_sink_pv(s, sinkcol, vband)
        for g in range(GQ_A):
            c0 = (kh * GQ_A + g) * HD_A
            mix_ref[:, c0:c0 + HD_A] = o[g * BLK:(g + 1) * BLK]

    kd_t = (p[:, C_KB:C_KB + H_B * DK_B] * kdec_ref[...]).T
    qd = p[:, C_QB:C_QB + H_B * DK_B] * qdec_ref[...]
    for h in range(H_B):
        q = p[:, C_QB + h * DK_B:C_QB + (h + 1) * DK_B]
        k = p[:, C_KB + h * DK_B:C_KB + (h + 1) * DK_B] * (DK_B ** -0.5)
        v = p[:, C_VB + h * DV_B:C_VB + (h + 1) * DV_B]
        a = _mm_nt(q, k) * dmat_ref[h]
        s0 = sret_ref[h]
        o = _mm(a, v) + _mm(qd[:, h * DK_B:(h + 1) * DK_B], s0)
        sret_ref[h] = s0 * scal_ref[l, H_A + h] + _mm(kd_t[h * DK_B:(h + 1) * DK_B], v)
        c0 = H_A * HD_A + h * DV_B
        mix_ref[:, c0:c0 + DV_B] = _group_norm_gate(o, p[:, C_GB + h * DV_B:C_GB + (h + 1) * DV_B])

    o_diag, qe, k2, ebl = _hgrn_front(p, lb_ref[0], None, tril_ref[...], sel_ref[...], onesbd_ref[...],
                                      b_ref, q_ref, SUB_P)
    k2_t, ebl_t = k2.T, ebl.T
    v = p[:, C_IC:C_IC + HC]
    bdf = bdf_ref[...]
    st = shg_ref[...]
    o_inter = []
    for i in range(BLK // SUB_P):
        o_inter.append(_mm(qe[i * SUB_P:(i + 1) * SUB_P], st))
        u = _mm(k2_t * _col_mask(BLK, SUB_P, i), v) * bdf
        st = ebl_t[:, i * SUB_P:i * SUB_P + 1] * st + u
    shg_ref[...] = st
    o = jnp.concatenate(o_inter, axis=0) + o_diag
    c0 = H_A * HD_A + H_B * DV_B
    mix_ref[:, c0:c0 + HC] = _hgrn_finish(o, p, gain_ref[0], bdf)

    @pl.when(t == pl.num_programs(1) - 1)
    def _():
        kkeep_ref[0] = p[:, C_KA:C_KA + KV_A]
        vkeep_ref[0] = p[:, C_VA:C_VA + KV_A]
        rets_ref[0] = sret_ref[...]
        hgs_ref[0] = shg_ref[...]


def _mixer_prompt(proj, scal, lb, gain, l, bp, t_len, tabs, htab):
    nb = t_len // BLK
    const2 = lambda b, t: (0, 0)
    const3 = lambda b, t: (0, 0, 0)
    layer3 = lambda b, t: (l, 0, 0)
    full = lambda a: pl.BlockSpec(a.shape, const2 if a.ndim == 2 else const3)
    return pl.pallas_call(
        functools.partial(_mixer_prompt_kernel, l=l),
        grid=(bp, nb),
        in_specs=[pl.BlockSpec(memory_space=pltpu.SMEM),
                  pl.BlockSpec((BLK, D_PROJ), lambda b, t: (b * nb + t, 0)),
                  pl.BlockSpec((BLK, 2 * KV_A), lambda b, t: (b * nb + jnp.maximum(t - 1, 0), 1)),
                  pl.BlockSpec((1, KVH_A, GQ_A * BLK, 2 * BLK), lambda b, t: (jnp.minimum(t, 1), 0, 0, 0)),
                  full(tabs["dmat"]), full(tabs["qdec"]), full(tabs["kdec"]),
                  pl.BlockSpec((1, 1, HC), layer3), pl.BlockSpec((1, 1, HC), layer3),
                  full(htab["tril"]), full(htab["sel"]), full(htab["onesbd"]), full(htab["bdf"])],
        out_specs=[pl.BlockSpec((BLK, D_MODEL), lambda b, t: (b * nb + t, 0)),
                   pl.BlockSpec((1, WINDOW, KV_A), lambda b, t: (b, 0, 0)),
                   pl.BlockSpec((1, WINDOW, KV_A), lambda b, t: (b, 0, 0)),
                   pl.BlockSpec((1, H_B, DK_B, DV_B), lambda b, t: (b, 0, 0, 0)),
                   pl.BlockSpec((1, HC, HC), lambda b, t: (b, 0, 0))],
        out_shape=[jax.ShapeDtypeStruct((bp * t_len, D_MODEL), F32),
                   jax.ShapeDtypeStruct((bp, WINDOW, KV_A), F32),
                   jax.ShapeDtypeStruct((bp, WINDOW, KV_A), F32),
                   jax.ShapeDtypeStruct((bp, H_B, DK_B, DV_B), F32),
                   jax.ShapeDtypeStruct((bp, HC, HC), F32)],
        scratch_shapes=[pltpu.VMEM((H_B, DK_B, DV_B), F32), pltpu.VMEM((HC, HC), F32),
                        pltpu.VMEM((BLK, HC), F32), pltpu.VMEM((BLK, HC), F32)],
        compiler_params=_cparams("parallel", "arbitrary"),
        name="mixer_prompt",
    )(scal, proj, proj, tabs["bias"], tabs["dmat"], tabs["qdec"], tabs["kdec"], lb, gain,
      htab["tril"], htab["sel"], htab["onesbd"], htab["bdf"])


def _mixer_sample_kernel(scal_ref, p_ref, kbuf_ref, vbuf_ref, rs_ref, hs_ref, bias_ref, dblk_ref, qdec_ref,
                         kdec_ref, lb_ref, gain_ref, padmask_ref, tril_ref, sel_ref, onesbd_ref, bdf_ref,
                         mix_ref, rso_ref, hso_ref, kall_ref, vall_ref, oint_ref, b_ref, q_ref, *, l):
    p = p_ref[...]
    half = lax.broadcasted_iota(I32, (GQ_A * T_PAD, 1), 0) < T_PAD
    kall_ref[...] = jnp.zeros_like(kall_ref)
    vall_ref[...] = jnp.zeros_like(vall_ref)

    for bb in range(GB):
        r0 = bb * T_PAD
        kall_ref[0:WINDOW] = kbuf_ref[bb]
        vall_ref[0:WINDOW] = vbuf_ref[bb]
        kall_ref[WINDOW:WINDOW + T_PAD] = p[r0:r0 + T_PAD, C_KA:C_KA + KV_A]
        vall_ref[WINDOW:WINDOW + T_PAD] = p[r0:r0 + T_PAD, C_VA:C_VA + KV_A]
        kall, vall = kall_ref[...], vall_ref[...]
        for kh in range(KVH_A):
            q = jnp.concatenate([p[r0:r0 + T_PAD, C_QA + (kh * GQ_A + g) * HD_A:C_QA + (kh * GQ_A + g + 1) * HD_A]
                                 for g in range(GQ_A)], axis=0)
            s = _mm_nt(q, kall[:, kh * HD_A:(kh + 1) * HD_A]) * (HD_A ** -0.5) + bias_ref[kh]
            sinkcol = jnp.where(half, scal_ref[l, kh * GQ_A], scal_ref[l, kh * GQ_A + 1])
            o = _softmax_sink_pv(s, sinkcol, vall[:, kh * HD_A:(kh + 1) * HD_A])
            for g in range(GQ_A):
                c0 = (kh * GQ_A + g) * HD_A
                mix_ref[r0:r0 + T_PAD, c0:c0 + HD_A] = o[g * T_PAD:(g + 1) * T_PAD]

    kd_t = (p[:, C_KB:C_KB + H_B * DK_B] * kdec_ref[...]).T
    qd = p[:, C_QB:C_QB + H_B * DK_B] * qdec_ref[...]
    for h in range(H_B):
        q = p[:, C_QB + h * DK_B:C_QB + (h + 1) * DK_B]
        k = p[:, C_KB + h * DK_B:C_KB + (h + 1) * DK_B] * (DK_B ** -0.5)
        v = p[:, C_VB + h * DV_B:C_VB + (h + 1) * DV_B]
        a = _mm_nt(q, k) * dblk_ref[h]
        o_intra = _mm(a, v)
        for bb in range(GB):
            r0 = bb * T_PAD
            s0 = rs_ref[bb, h]
            oint_ref[r0:r0 + T_PAD, 0:DV_B] = _mm(qd[r0:r0 + T_PAD, h * DK_B:(h + 1) * DK_B], s0)
            rso_ref[bb, h] = (s0 * scal_ref[l, H_A + h]
                              + _mm(kd_t[h * DK_B:(h + 1) * DK_B] * _col_mask(BLK, T_PAD, bb), v))
        o = o_intra + oint_ref[:, 0:DV_B]
        c0 = H_A * HD_A + h * DV_B
        mix_ref[:, c0:c0 + DV_B] = _group_norm_gate(o, p[:, C_GB + h * DV_B:C_GB + (h + 1) * DV_B])

    o_diag, qe, k2, ebl = _hgrn_front(p, lb_ref[0], padmask_ref[...], tril_ref[...], sel_ref[...], onesbd_ref[...],
                                      b_ref, q_ref, T_PAD)
    k2_t, ebl_t = k2.T, ebl.T
    v = p[:, C_IC:C_IC + HC]
    for bb in range(GB):
        r0 = bb * T_PAD
        k2m = k2_t * _col_mask(BLK, T_PAD, bb)
        for h in range(H_C):
            s0 = hs_ref[bb, h]
            oint_ref[r0:r0 + T_PAD, h * DV_C:(h + 1) * DV_C] = _mm(qe[r0:r0 + T_PAD, h * DK_C:(h + 1) * DK_C], s0)
            hso_ref[bb, h] = (ebl_t[h * DK_C:(h + 1) * DK_C, r0:r0 + 1] * s0
                              + _mm(k2m[h * DK_C:(h + 1) * DK_C], v[:, h * DV_C:(h + 1) * DV_C]))
    o = oint_ref[...] + o_diag
    c0 = H_A * HD_A + H_B * DV_B
    mix_ref[:, c0:c0 + HC] = _hgrn_finish(o, p, gain_ref[0], bdf_ref[...])


def _mixer_sample(proj8, scal, kbuf, vbuf, rstate, hstate, lb, gain, l, bs, tabs, htab):
    nb = bs // GB
    full = lambda a: pl.BlockSpec(a.shape, (lambda i: (0, 0)) if a.ndim == 2 else (lambda i: (0, 0, 0)))
    layer3 = lambda i: (l, 0, 0)
    return pl.pallas_call(
        functools.partial(_mixer_sample_kernel, l=l),
        grid=(nb,),
        in_specs=[pl.BlockSpec(memory_space=pltpu.SMEM),
                  pl.BlockSpec((BLK, D_PROJ), lambda i: (i, 0)),
                  pl.BlockSpec((GB, WINDOW, KV_A), lambda i: (l * nb + i, 0, 0)),
                  pl.BlockSpec((GB, WINDOW, KV_A), lambda i: (l * nb + i, 0, 0)),
                  pl.BlockSpec((GB, H_B, DK_B, DV_B), lambda i: (l * nb + i, 0, 0, 0)),
                  pl.BlockSpec((GB, H_C, DK_C, DV_C), lambda i: (l * nb + i, 0, 0, 0)),
                  full(tabs["bias"]), full(tabs["dblk"]), full(tabs["qdec"]), full(tabs["kdec"]),
                  pl.BlockSpec((1, 1, HC), layer3), pl.BlockSpec((1, 1, HC), layer3), full(tabs["padmask"]),
                  full(htab["tril"]), full(htab["sel"]), full(htab["onesbd"]), full(htab["bdf"])],
        out_specs=[pl.BlockSpec((BLK, D_MODEL), lambda i: (i, 0)),
                   pl.BlockSpec((GB, H_B, DK_B, DV_B), lambda i: (i, 0, 0, 0)),
                   pl.BlockSpec((GB, H_C, DK_C, DV_C), lambda i: (i, 0, 0, 0))],
        out_shape=[jax.ShapeDtypeStruct((bs * T_PAD, D_MODEL), F32),
                   jax.ShapeDtypeStruct((bs, H_B, DK_B, DV_B), F32),
                   jax.ShapeDtypeStruct((bs, H_C, DK_C, DV_C), F32)],
        scratch_shapes=[pltpu.VMEM((2 * WINDOW, KV_A), F32), pltpu.VMEM((2 * WINDOW, KV_A), F32),
                        pltpu.VMEM((BLK, HC), F32), pltpu.VMEM((BLK, HC), F32), pltpu.VMEM((BLK, HC), F32)],
        compiler_params=_cparams("parallel"),
        name="mixer_sample",
    )(scal, proj8, kbuf, vbuf, rstate, hstate, tabs["bias"], tabs["dblk"], tabs["qdec"], tabs["kdec"],
      lb, gain, tabs["padmask"], htab["tril"], htab["sel"], htab["onesbd"], htab["bdf"])


def _attend_memory(q, mk, mv):
    outs = []
    for h in range(XH):
        sl = slice(h * XHD, (h + 1) * XHD)
        s = _mm_nt(q[:, sl], mk[:, sl]) * (XHD ** -0.5)
        e = jnp.exp(s - jnp.max(s, axis=-1, keepdims=True))
        outs.append(_mm(e, mv[:, sl]) / jnp.sum(e, axis=-1, keepdims=True))
    return jnp.concatenate(outs, axis=-1)


def _xattn_prompt_kernel(x_ref, g_ref, wq_ref, wo_ref, mk_ref, mv_ref, o_ref):
    x = x_ref[...]
    q = jnp.dot(_rms(x, g_ref[0]).astype(BF16), wq_ref[0], preferred_element_type=F32)
    o = _attend_memory(q, mk_ref[...], mv_ref[...])
    o_ref[...] = x + jnp.dot(o.astype(BF16), wo_ref[0], preferred_element_type=F32)


def _xattn_prompt(x, g, wq, wo, mk, mv, l, bp, t_len):
    tq = _pick(t_len, (512, 256, 128))
    nt = t_len // tq
    dq = XH * XHD
    layer3 = lambda b, t: (l, 0, 0)
    return pl.pallas_call(
        _xattn_prompt_kernel,
        grid=(bp, nt),
        in_specs=[pl.BlockSpec((tq, D_MODEL), lambda b, t: (b * nt + t, 0)),
                  pl.BlockSpec((1, 1, D_MODEL), layer3),
                  pl.BlockSpec((1, D_MODEL, dq), layer3),
                  pl.BlockSpec((1, dq, D_MODEL), layer3),
                  pl.BlockSpec((N_MEM, dq), lambda b, t: (b, 0)),
                  pl.BlockSpec((N_MEM, dq), lambda b, t: (b, 0))],
        out_specs=pl.BlockSpec((tq, D_MODEL), lambda b, t: (b * nt + t, 0)),
        out_shape=jax.ShapeDtypeStruct(x.shape, F32),
        input_output_aliases={0: 0},
        compiler_params=_cparams("parallel", "parallel"),
        name="xattn_prompt",
    )(x, g, wq, wo, mk, mv)


XB = 8


def _xattn_sample_kernel(q_ref, mk_ref, mv_ref, o_ref):
    for bb in range(XB):
        r0 = bb * T_PAD
        o_ref[r0:r0 + T_PAD] = _attend_memory(q_ref[r0:r0 + T_PAD], mk_ref[bb], mv_ref[bb])


def _xattn_sample(q8, mk, mv, l, bs):
    dq = XH * XHD
    nb = bs // XB
    return pl.pallas_call(
        _xattn_sample_kernel,
        grid=(nb,),
        in_specs=[pl.BlockSpec((XB * T_PAD, dq), lambda i: (i, 0)),
                  pl.BlockSpec((XB, N_MEM, dq), lambda i: (l * nb + i, 0, 0)),
                  pl.BlockSpec((XB, N_MEM, dq), lambda i: (l * nb + i, 0, 0))],
        out_specs=pl.BlockSpec((XB * T_PAD, dq), lambda i: (i, 0)),
        out_shape=jax.ShapeDtypeStruct((bs * T_PAD, dq), F32),
        compiler_params=_cparams("parallel"),
        name="xattn_sample",
    )(q8, mk, mv)


def _ffn_kernel(be_ref, nused_ref, x_ref, g_ref, w1_ref, w3_ref, w2_ref, o_ref, xn_ref, *, dense):
    i, j = pl.program_id(0), pl.program_id(1)

    @pl.when(j == 0)
    def _():
        x = x_ref[...]
        if dense:
            xn_ref[...] = _rms(x, g_ref[0]).astype(BF16)
            o_ref[...] = x
        else:
            xn_ref[...] = x.astype(BF16)
            o_ref[...] = jnp.zeros_like(o_ref)

    @pl.when(i < nused_ref[0])
    def _():
        xn = xn_ref[...]
        h1 = jnp.dot(xn, w1_ref[0], preferred_element_type=F32)
        h3 = jnp.dot(xn, w3_ref[0], preferred_element_type=F32)
        o_ref[...] += jnp.dot((_silu(h1) * h3).astype(BF16), w2_ref[0], preferred_element_type=F32)


def _ffn(x, g, l, w1, w3, w2, block_e, nused, tm, dense):
    rows = x.shape[0]
    dff = w1.shape[2]
    tf = _pick(dff, (512, 256, 128))
    nf = dff // tf

    def xmap(i, j, be, nu):
        return (jnp.minimum(i, nu[0] - 1), 0)

    def w13map(i, j, be, nu):
        return (be[jnp.minimum(i, nu[0] - 1)], 0, jnp.where(i < nu[0], j, nf - 1))

    def w2map(i, j, be, nu):
        return (be[jnp.minimum(i, nu[0] - 1)], jnp.where(i < nu[0], j, nf - 1), 0)

    grid_spec = pltpu.PrefetchScalarGridSpec(
        num_scalar_prefetch=2,
        grid=(rows // tm, nf),
        in_specs=[pl.BlockSpec((tm, D_MODEL), xmap),
                  pl.BlockSpec((1, 1, D_MODEL), lambda i, j, be, nu: (l, 0, 0)),
                  pl.BlockSpec((1, D_MODEL, tf), w13map),
                  pl.BlockSpec((1, D_MODEL, tf), w13map),
                  pl.BlockSpec((1, tf, D_MODEL), w2map)],
        out_specs=pl.BlockSpec((tm, D_MODEL), lambda i, j, be, nu: (i, 0)),
        scratch_shapes=[pltpu.VMEM((tm, D_MODEL), BF16)],
    )
    return pl.pallas_call(
        functools.partial(_ffn_kernel, dense=dense),
        grid_spec=grid_spec,
        out_shape=jax.ShapeDtypeStruct((rows, D_MODEL), F32),
        input_output_aliases={2: 0} if dense else {},
        compiler_params=_cparams("arbitrary", "arbitrary"),
        name="ffn_dense" if dense else "ffn_experts",
    )(block_e, nused, x, g, w1, w3, w2)


def _router_kernel(x_ref, g_ref, wr_ref, h_ref, idx_ref, gate_ref):
    h = _rms(x_ref[...], g_ref[0])
    h_ref[...] = h
    logits = _mm_hi(h, wr_ref[...])
    lane = lax.broadcasted_iota(I32, logits.shape, 1)
    logits = jnp.where(lane < N_EXPERTS, logits, NEG_INF)
    m1 = jnp.max(logits, axis=-1, keepdims=True)
    i1 = jnp.min(jnp.where(logits == m1, lane, LANES), axis=-1, keepdims=True)
    rest = jnp.where(lane == i1, NEG_INF, logits)
    m2 = jnp.max(rest, axis=-1, keepdims=True)
    i2 = jnp.min(jnp.where(rest == m2, lane, LANES), axis=-1, keepdims=True)
    e2 = jnp.exp(m2 - m1)
    den = 1.0 + e2
    idx_ref[...] = jnp.where(lane == 0, i1, jnp.where(lane == 1, i2, 0))
    gate_ref[...] = jnp.where(lane == 0, 1.0 / den, jnp.where(lane == 1, e2 / den, 0.0))


def _router(x, g, l, w_router):
    n = x.shape[0]
    tm = _pick(n, ROW_TILES)
    wr = jnp.pad(w_router, ((0, 0), (0, LANES - N_EXPERTS)))
    return pl.pallas_call(
        _router_kernel,
        grid=(n // tm,),
        in_specs=[pl.BlockSpec((tm, D_MODEL), lambda i: (i, 0)),
                  pl.BlockSpec((1, 1, D_MODEL), lambda i: (l, 0, 0)),
                  pl.BlockSpec((D_MODEL, LANES), lambda i: (0, 0))],
        out_specs=[pl.BlockSpec((tm, D_MODEL), lambda i: (i, 0)),
                   pl.BlockSpec((tm, LANES), lambda i: (i, 0)),
                   pl.BlockSpec((tm, LANES), lambda i: (i, 0))],
        out_shape=[jax.ShapeDtypeStruct((n, D_MODEL), F32),
                   jax.ShapeDtypeStruct((n, LANES), I32),
                   jax.ShapeDtypeStruct((n, LANES), F32)],
        compiler_params=_cparams("parallel"),
        name="router",
    )(x, g, wr)


GATHER_UNROLL = 8


def _gather_rows_kernel(tok_ref, src_ref, o_ref, sem, *, tg):
    base = pl.program_id(0) * tg

    def issue(r, c):
        pltpu.make_async_copy(src_ref.at[pl.ds(tok_ref[base + r], 1)], o_ref.at[pl.ds(r, 1)], sem).start()
        return c

    lax.fori_loop(0, tg, issue, 0, unroll=GATHER_UNROLL)
    pltpu.make_async_copy(src_ref.at[pl.ds(0, tg)], o_ref, sem).wait()


def _gather_rows(src, tok, tg):
    rows = tok.shape[0]
    grid_spec = pltpu.PrefetchScalarGridSpec(
        num_scalar_prefetch=1,
        grid=(rows // tg,),
        in_specs=[pl.BlockSpec(memory_space=pl.ANY)],
        out_specs=pl.BlockSpec((tg, D_MODEL), lambda i, tok: (i, 0)),
        scratch_shapes=[pltpu.SemaphoreType.DMA],
    )
    return pl.pallas_call(
        functools.partial(_gather_rows_kernel, tg=tg),
        grid_spec=grid_spec,
        out_shape=jax.ShapeDtypeStruct((rows, D_MODEL), F32),
        compiler_params=_cparams("arbitrary"),
        name="moe_gather",
    )(tok, src)


def _combine_kernel(pos_ref, x_ref, gate_ref, y_ref, o_ref, buf_ref, sem, *, tc):
    base = pl.program_id(0) * tc

    def issue(r, c):
        for k in range(TOP_K):
            pltpu.make_async_copy(y_ref.at[pl.ds(pos_ref[(base + r) * TOP_K + k], 1)],
                                  buf_ref.at[k, pl.ds(r, 1)], sem).start()
        return c

    lax.fori_loop(0, tc, issue, 0, unroll=GATHER_UNROLL)
    for k in range(TOP_K):
        pltpu.make_async_copy(y_ref.at[pl.ds(0, tc)], buf_ref.at[k], sem).wait()
    gate = gate_ref[...]
    o_ref[...] = x_ref[...] + (buf_ref[0] * gate[:, 0:1] + buf_ref[1] * gate[:, 1:2])


def _combine(x, gate, y, pos, tc):
    n = x.shape[0]
    grid_spec = pltpu.PrefetchScalarGridSpec(
        num_scalar_prefetch=1,
        grid=(n // tc,),
        in_specs=[pl.BlockSpec((tc, D_MODEL), lambda i, pos: (i, 0)),
                  pl.BlockSpec((tc, LANES), lambda i, pos: (i, 0)),
                  pl.BlockSpec(memory_space=pl.ANY)],
        out_specs=pl.BlockSpec((tc, D_MODEL), lambda i, pos: (i, 0)),
        scratch_shapes=[pltpu.VMEM((TOP_K, tc, D_MODEL), F32), pltpu.SemaphoreType.DMA],
    )
    return pl.pallas_call(
        functools.partial(_combine_kernel, tc=tc),
        grid_spec=grid_spec,
        out_shape=jax.ShapeDtypeStruct(x.shape, F32),
        input_output_aliases={1: 0},
        compiler_params=_cparams("arbitrary"),
        name="moe_combine",
    )(pos, x, gate, y)


def _moe(x, g, l, w_router, w1, w3, w2, e0):
    n = x.shape[0]
    tm = _pick(n, ROW_TILES)
    h, idx, gate = _router(x, g, l, w_router)
    a = n * TOP_K
    e_flat = idx[:, :TOP_K].reshape(a)
    onehot = (e_flat[:, None] == jnp.arange(N_EXPERTS, dtype=I32)[None, :]).astype(I32)
    csum = jnp.cumsum(onehot, axis=0)
    counts = csum[-1]
    rank = jnp.sum((csum - onehot) * onehot, axis=1)
    padded = (counts + tm - 1) // tm * tm
    pend = jnp.cumsum(padded)
    pstart = pend - padded
    dest = (pstart[e_flat] + rank).astype(I32)
    nb = a // tm + N_EXPERTS
    tok = jnp.zeros((nb * tm,), I32).at[dest].set(jnp.repeat(jnp.arange(n, dtype=I32), TOP_K))
    block_start = jnp.arange(nb, dtype=I32) * tm
    block_e = jnp.minimum(jnp.sum((pend[None, :] <= block_start[:, None]).astype(I32), axis=1), N_EXPERTS - 1)
    nused = (pend[-1:] // tm).astype(I32)
    xg = _gather_rows(h, tok, tm)
    y = _ffn(xg, g, l, w1, w3, w2, (block_e + e0).astype(I32), nused, tm, dense=False)
    return _combine(x, gate, y, dest, _pick(n, (256, 128, 64, 32, 16, 8)))


def kernel(x_prompt, x_sample, cache_swa_k, cache_swa_v, state_ret, state_hgrn, cache_mem_k, cache_mem_v, mem_prompt, norm_mix, w_in, swa_sinks, hgrn_lb_logits, hgrn_norm, w_out, norm_xattn, norm_mem, wx_q, wx_k, wx_v, wx_o, norm_ffn, ffn_w1, ffn_w3, ffn_w2, moe_router, moe_w1, moe_w3, moe_w2, final_norm):
    bp, t_len, _ = x_prompt.shape
    bs, ts, _ = x_sample.shape
    depth = w_in.shape[0]
    n_p, n_s = bp * t_len, bs * ts
    n = n_p + n_s
    assert t_len % BLK == 0 and bs % GB == 0 and bs % XB == 0 and ts <= T_PAD
    assert cache_swa_k.shape[2] == WINDOW and ts < WINDOW
    dq = XH * XHD

    x = jnp.concatenate([x_prompt.reshape(n_p, D_MODEL), x_sample.reshape(n_s, D_MODEL)], axis=0)
    ptab, stab = _prompt_tables(), _sample_tables(ts)
    htab_p, htab_s = _hgrn_tables(SUB_P), _hgrn_tables(T_PAD)
    lb_p = jax.nn.softmax(hgrn_lb_logits.astype(F32), axis=0)
    lower_bounds = (jnp.cumsum(lb_p, axis=0) - lb_p[0]).reshape(depth, 1, HC)
    gains = hgrn_norm.astype(F32).reshape(depth, 1, HC)
    sinks = swa_sinks.astype(F32)
    scal_p = jnp.concatenate([sinks, jnp.broadcast_to(jnp.asarray(ptab["cdec"], F32), (depth, H_B))], axis=1)
    scal_s = jnp.concatenate([sinks, jnp.broadcast_to(jnp.asarray(stab["cdec"], F32), (depth, H_B))], axis=1)
    mem = mem_prompt.reshape(bp * N_MEM, D_MODEL)

    bf = lambda w: w.astype(BF16)
    w_in_b, w_out_b = bf(w_in), bf(w_out)
    wq_b, wk_b, wv_b, wo_b = bf(wx_q), bf(wx_k), bf(wx_v), bf(wx_o)
    f1_b, f3_b, f2_b = bf(ffn_w1), bf(ffn_w3), bf(ffn_w2)
    m1_b = bf(moe_w1).reshape((-1,) + moe_w1.shape[2:])
    m3_b = bf(moe_w3).reshape((-1,) + moe_w3.shape[2:])
    m2_b = bf(moe_w2).reshape((-1,) + moe_w2.shape[2:])
    vec3 = lambda v: v.astype(F32).reshape(depth, 1, D_MODEL)
    n_mix, n_xat, n_mem, n_ffn = vec3(norm_mix), vec3(norm_xattn), vec3(norm_mem), vec3(norm_ffn)

    kbuf = cache_swa_k.astype(F32).reshape(depth * bs, WINDOW, KV_A)
    vbuf = cache_swa_v.astype(F32).reshape(depth * bs, WINDOW, KV_A)
    rstate = state_ret.astype(F32).reshape((depth * bs,) + state_ret.shape[2:])
    hstate = state_hgrn.astype(F32).reshape((depth * bs,) + state_hgrn.shape[2:])
    cmk = cache_mem_k.astype(F32).reshape(depth * bs, N_MEM, dq)
    cmv = cache_mem_v.astype(F32).reshape(depth * bs, N_MEM, dq)

    def pad_tokens(a):
        return jnp.pad(a.reshape(bs, ts, -1), ((0, 0), (0, T_PAD - ts), (0, 0))).reshape(bs * T_PAD, -1)

    def unpad_tokens(a):
        return a.reshape(bs, T_PAD, -1)[:, :ts].reshape(bs * ts, -1)

    outs = {k: [] for k in ("pk", "pv", "pr", "pc", "pmk", "pmv", "sk", "sv", "sr", "sc")}
    for l in range(depth):
        proj = _rms_matmul(x, n_mix, w_in_b, l)
        mix_p, kkeep, vkeep, ret_p, hg_p = _mixer_prompt(proj, scal_p, lower_bounds, gains, l, bp, t_len, ptab, htab_p)
        proj_s = lax.slice_in_dim(proj, n_p, n, axis=0)
        mix_s, ret_s, hg_s = _mixer_sample(pad_tokens(proj_s), scal_s, kbuf, vbuf, rstate, hstate,
                                           lower_bounds, gains, l, bs, stab, htab_s)
        x = _matmul_res(mix_p, w_out_b, l, x, 0)
        x = _matmul_res(unpad_tokens(mix_s), w_out_b, l, x, n_p)

        outs["pk"].append(kkeep.reshape(bp, WINDOW, KVH_A, HD_A))
        outs["pv"].append(vkeep.reshape(bp, WINDOW, KVH_A, HD_A))
        outs["pr"].append(ret_p)
        outs["pc"].append(jnp.stack([hg_p[:, h * DK_C:(h + 1) * DK_C, h * DV_C:(h + 1) * DV_C]
                                     for h in range(H_C)], axis=1))
        new_k = proj_s[:, C_KA:C_KA + KV_A].reshape(bs, ts, KVH_A, HD_A)
        new_v = proj_s[:, C_VA:C_VA + KV_A].reshape(bs, ts, KVH_A, HD_A)
        outs["sk"].append(jnp.concatenate([cache_swa_k[l][:, ts:].astype(F32), new_k], axis=1))
        outs["sv"].append(jnp.concatenate([cache_swa_v[l][:, ts:].astype(F32), new_v], axis=1))
        outs["sr"].append(ret_s)
        outs["sc"].append(hg_s)

        mk = _rms_matmul(mem, n_mem, wk_b, l)
        mv = _rms_matmul(mem, n_mem, wv_b, l)
        outs["pmk"].append(mk.reshape(bp, N_MEM, XH, XHD))
        outs["pmv"].append(mv.reshape(bp, N_MEM, XH, XHD))
        x = _xattn_prompt(x, n_xat, wq_b, wo_b, mk, mv, l, bp, t_len)
        q_s = _rms_matmul(x, n_xat, wq_b, l, row_off=n_p, rows=n_s)
        o_s = _xattn_sample(pad_tokens(q_s), cmk, cmv, l, bs)
        x = _matmul_res(unpad_tokens(o_s), wo_b, l, x, n_p)

        e = l // 2
        if l % 2 == 0:
            tm = _pick(n, (1536, 768) + ROW_TILES)
            nb = n // tm
            x = _ffn(x, n_ffn, l, f1_b, f3_b, f2_b, jnp.full((nb,), e, I32), jnp.full((1,), nb, I32), tm, dense=True)
        else:
            x = _moe(x, n_ffn, l, moe_router[e], m1_b, m3_b, m2_b, e * N_EXPERTS)

    y_p = _rmsnorm_rows(x, final_norm, 0, n_p).reshape(bp, t_len, D_MODEL)
    y_s = _rmsnorm_rows(x, final_norm, n_p, n_s).reshape(bs, ts, D_MODEL)
    st = lambda k: jnp.stack(outs[k])
    return (y_p, y_s, st("pk"), st("pv"), st("pr"), st("pc"), st("pmk"), st("pmv"),
            st("sk"), st("sv"), st("sr"), st("sc"))
```

```python
import functools

import numpy as np
import jax
import jax.numpy as jnp
from jax import lax
from jax.experimental import pallas as pl
from jax.experimental.pallas import tpu as pltpu

F32, BF16, I32 = jnp.float32, jnp.bfloat16, jnp.int32
HIGHEST = lax.Precision.HIGHEST
NEG_INF = float("-inf")

D_MODEL = 1024
WINDOW = 128
H_A, KVH_A, GQ_A, HD_A = 4, 2, 2, 64
ALIBI_MAX_EXP = 8.0
H_B, DK_B, DV_B = 4, 64, 128
RET_CHUNK = 128
H_C, DK_C, DV_C = 4, 64, 64
N_MEM, XH, XHD = 256, 4, 128
N_EXPERTS, TOP_K = 8, 2
EPS = 1e-6
D_PROJ = 3072
C_QA, C_KA, C_VA = 0, 256, 384
C_QB, C_KB, C_VB, C_GB = 512, 768, 1024, 1536
C_FC, C_QC, C_IC, C_GC = 2048, 2304, 2560, 2816
KV_A = KVH_A * HD_A
HC = H_C * DK_C

BLK = 128
SUB_P = 16
T_PAD = 8
GB = BLK // T_PAD
LANES = 128
VMEM_LIMIT = 56 * 1024 * 1024
ROW_TILES = (512, 256, 128, 64, 32, 16, 8)


def _pick(n, cands):
    for c in cands:
        if n % c == 0:
            return c
    raise ValueError(f"no tile for {n}")


def _row_tile(rows, row_off):
    return _pick(int(np.gcd(rows, row_off)) if row_off else rows, ROW_TILES)


def _cparams(*sem):
    return pltpu.CompilerParams(dimension_semantics=sem, vmem_limit_bytes=VMEM_LIMIT)


def _mm(a, b):
    return jnp.dot(a.astype(BF16), b.astype(BF16), preferred_element_type=F32)


def _mm_nt(a, b):
    return lax.dot_general(a.astype(BF16), b.astype(BF16), (((1,), (1,)), ((), ())),
                           preferred_element_type=F32)


def _mm_hi(a, b):
    return jnp.dot(a, b, precision=HIGHEST, preferred_element_type=F32)


def _sigmoid(x):
    return 1.0 / (1.0 + jnp.exp(-x))


def _silu(x):
    return x * _sigmoid(x)


def _rms(x, g):
    return x * lax.rsqrt(jnp.mean(x * x, axis=-1, keepdims=True) + EPS) * g


def _rms_matmul_kernel(x_ref, g_ref, w_ref, o_ref, xn_ref):
    @pl.when(pl.program_id(1) == 0)
    def _():
        xn_ref[...] = _rms(x_ref[...], g_ref[0]).astype(BF16)

    o_ref[...] = jnp.dot(xn_ref[...], w_ref[0], preferred_element_type=F32)


def _rms_matmul(x, g, w, l, row_off=0, rows=None):
    n, k = x.shape
    m = w.shape[2]
    rows = n if rows is None else rows
    tm = _row_tile(rows, row_off)
    tn = _pick(m, (3072, 1536, 1024, 512, 256, 128))
    off = row_off // tm
    return pl.pallas_call(
        _rms_matmul_kernel,
        grid=(rows // tm, m // tn),
        in_specs=[pl.BlockSpec((tm, k), lambda i, j: (i + off, 0)),
                  pl.BlockSpec((1, 1, k), lambda i, j: (l, 0, 0)),
                  pl.BlockSpec((1, k, tn), lambda i, j: (l, 0, j))],
        out_specs=pl.BlockSpec((tm, tn), lambda i, j: (i, j)),
        out_shape=jax.ShapeDtypeStruct((rows, m), F32),
        scratch_shapes=[pltpu.VMEM((tm, k), BF16)],
        compiler_params=_cparams("parallel", "arbitrary"),
        name="rms_matmul",
    )(x, g, w)


def _matmul_res_kernel(a_ref, w_ref, x_ref, o_ref):
    o_ref[...] = x_ref[...] + jnp.dot(a_ref[...].astype(BF16), w_ref[0], preferred_element_type=F32)


def _matmul_res(a, w, l, x, row_off):
    rows, k = a.shape
    m = w.shape[2]
    tm = _row_tile(rows, row_off)
    off = row_off // tm
    return pl.pallas_call(
        _matmul_res_kernel,
        grid=(rows // tm,),
        in_specs=[pl.BlockSpec((tm, k), lambda i: (i, 0)),
                  pl.BlockSpec((1, k, m), lambda i: (l, 0, 0)),
                  pl.BlockSpec((tm, m), lambda i: (i + off, 0))],
        out_specs=pl.BlockSpec((tm, m), lambda i: (i + off, 0)),
        out_shape=jax.ShapeDtypeStruct(x.shape, F32),
        input_output_aliases={2: 0},
        compiler_params=_cparams("parallel"),
        name="matmul_res",
    )(a, w, x)


def _rmsnorm_kernel(x_ref, g_ref, o_ref):
    o_ref[...] = _rms(x_ref[...], g_ref[...])


def _rmsnorm_rows(x, g, row_off, rows):
    n, k = x.shape
    tm = _row_tile(rows, row_off)
    off = row_off // tm
    return pl.pallas_call(
        _rmsnorm_kernel,
        grid=(rows // tm,),
        in_specs=[pl.BlockSpec((tm, k), lambda i: (i + off, 0)),
                  pl.BlockSpec((1, k), lambda i: (0, 0))],
        out_specs=pl.BlockSpec((tm, k), lambda i: (i, 0)),
        out_shape=jax.ShapeDtypeStruct((rows, k), F32),
        compiler_params=_cparams("parallel"),
        name="final_rmsnorm",
    )(x, g.reshape(1, k))


def _retention_gamma():
    return 1.0 - np.exp2(-5.0 - np.arange(H_B, dtype=np.float64))


def _alibi_slopes():
    return np.exp2(-(ALIBI_MAX_EXP / H_A) * np.arange(1, H_A + 1, dtype=np.float64))


def _head_cols(vals_hr, width):
    return np.repeat(vals_hr.T[:, :, None], width, axis=2).reshape(vals_hr.shape[1], -1)


def _hgrn_tables(sub):
    r = np.arange(BLK)
    same = (r[:, None] // sub) == (r[None, :] // sub)
    tril = (same & (r[None, :] <= r[:, None])).astype(np.float32)
    pr = np.arange(BLK * sub)
    sel = (r[:, None] == (pr // sub)[None, :]).astype(np.float32)
    hh = np.arange(HC) // DK_C
    bd = (hh[:, None] == hh[None, :]).astype(np.float32)
    return dict(tril=jnp.asarray(tril), sel=jnp.asarray(sel, dtype=BF16),
                onesbd=jnp.asarray(bd, dtype=BF16), bdf=jnp.asarray(bd))


def _prompt_tables():
    gam = _retention_gamma()
    i = np.arange(RET_CHUNK, dtype=np.float64)
    diff = i[:, None] - i[None, :]
    dmat = np.where(diff >= 0, gam[:, None, None] ** np.maximum(diff, 0.0), 0.0)
    qdec = _head_cols(gam[:, None] ** (i + 1.0), DK_B)
    kdec = _head_cols(gam[:, None] ** (RET_CHUNK - 1.0 - i), DK_B) * (DK_B ** -0.5)
    cdec = gam ** RET_CHUNK
    slopes = _alibi_slopes().reshape(KVH_A, GQ_A)
    qi = np.tile(np.arange(WINDOW), GQ_A)[:, None]
    qg = np.repeat(np.arange(GQ_A), WINDOW)[:, None]
    j = np.arange(2 * WINDOW)[None, :]
    rel = WINDOW + qi - j
    ok = (rel >= 0) & (rel < WINDOW)
    bias = np.stack([np.where(ok, -slopes[kh][qg] * rel, NEG_INF) for kh in range(KVH_A)])
    bias_first = np.where(j < WINDOW, NEG_INF, bias)
    return dict(dmat=jnp.asarray(dmat, dtype=F32), qdec=jnp.asarray(qdec, dtype=F32),
                kdec=jnp.asarray(kdec, dtype=F32), cdec=cdec,
                bias=jnp.asarray(np.stack([bias_first, bias]), dtype=F32))


def _sample_tables(t):
    gam = _retention_gamma()
    r = np.arange(BLK)
    ti, bi = r % T_PAD, r // T_PAD
    diff = (ti[:, None] - ti[None, :]).astype(np.float64)
    same = bi[:, None] == bi[None, :]
    dblk = np.where(same & (diff >= 0), gam[:, None, None] ** np.maximum(diff, 0.0), 0.0)
    qdec = _head_cols(gam[:, None] ** (ti + 1.0), DK_B)
    kdec = _head_cols(gam[:, None] ** (t - 1.0 - ti), DK_B) * (DK_B ** -0.5)
    cdec = gam ** t
    slopes = _alibi_slopes().reshape(KVH_A, GQ_A)
    qt = np.tile(np.arange(T_PAD), GQ_A)[:, None]
    qg = np.repeat(np.arange(GQ_A), T_PAD)[:, None]
    c = np.arange(2 * WINDOW)[None, :]
    rel = qt + WINDOW - c
    ok = (rel >= 0) & (rel < WINDOW) & (c < WINDOW + t)
    bias = np.stack([np.where(ok, -slopes[kh][qg] * rel, NEG_INF) for kh in range(KVH_A)])
    padmask = np.repeat((ti < t).astype(np.float32)[:, None], HC, axis=1)
    return dict(dblk=jnp.asarray(dblk, dtype=F32), qdec=jnp.asarray(qdec, dtype=F32),
                kdec=jnp.asarray(kdec, dtype=F32), cdec=cdec, bias=jnp.asarray(bias, dtype=F32),
                padmask=jnp.asarray(padmask))


def _tile_j(x, sub):
    nsub = x.shape[0] // sub
    y = jnp.broadcast_to(x.reshape(nsub, 1, sub, x.shape[1]), (nsub, sub, sub, x.shape[1]))
    return y.reshape(nsub * sub * sub, x.shape[1])


def _row_bcast(ref, r, sub):
    return jnp.broadcast_to(ref[r:r + 1, :], (sub, ref.shape[1]))


def _hgrn_front(p, lb, padmask, tril, sel, onesbd, b_ref, q_ref, sub):
    fc, qc, ic = p[:, C_FC:C_FC + HC], p[:, C_QC:C_QC + HC], p[:, C_IC:C_IC + HC]
    f = lb + (1.0 - lb) * _sigmoid(fc)
    kk = 1.0 - f
    g = jnp.log(f)
    if padmask is not None:
        kk = kk * padmask
        g = g * padmask
    qq = _silu(qc)
    b = _mm_hi(tril, g)
    b_ref[...] = b
    q_ref[...] = qq
    jrow = lax.broadcasted_iota(I32, (sub, 1), 0)
    bi = jnp.concatenate([_row_bcast(b_ref, r, sub) for r in range(BLK)], axis=0)
    qi = jnp.concatenate([jnp.where(jrow <= r % sub, _row_bcast(q_ref, r, sub), 0.0) for r in range(BLK)], axis=0)
    w = jnp.exp(jnp.minimum(bi - _tile_j(b, sub), 0.0))
    wm = qi * _tile_j(kk, sub) * w
    ab = jnp.dot(wm.astype(BF16), onesbd, preferred_element_type=F32)
    cm = ab * _tile_j(ic, sub)
    o_diag = jnp.dot(sel, cm.astype(BF16), preferred_element_type=F32)
    blrep = jnp.concatenate([_row_bcast(b_ref, i * sub + sub - 1, sub) for i in range(BLK // sub)], axis=0)
    return o_diag, qq * jnp.exp(b), kk * jnp.exp(blrep - b), jnp.exp(blrep)


def _hgrn_finish(o, p, gain, bdf):
    ms = _mm_hi(o * o, bdf) * (1.0 / DV_C)
    return o * lax.rsqrt(ms + EPS) * gain * _silu(p[:, C_GC:C_GC + HC])


def _group_norm_gate(o, gate):
    c = o - jnp.mean(o, axis=-1, keepdims=True)
    return c * lax.rsqrt(jnp.mean(c * c, axis=-1, keepdims=True) + EPS) * _silu(gate)


def _softmax_sink_pv(s, sinkcol, v):
    m = jnp.maximum(jnp.max(s, axis=-1, keepdims=True), sinkcol)
    e = jnp.exp(s - m)
    den = jnp.sum(e, axis=-1, keepdims=True) + jnp.exp(sinkcol - m)
    return _mm(e, v) / den


def _col_mask(width, sub, idx):
    return (lax.broadcasted_iota(I32, (1, width), 1) // sub == idx).astype(F32)


def _mixer_prompt_kernel(scal_ref, p_ref, prev_ref, bias_ref, dmat_ref, qdec_ref, kdec_ref, lb_ref, gain_ref,
                         tril_ref, sel_ref, onesbd_ref, bdf_ref,
                         mix_ref, kkeep_ref, vkeep_ref, rets_ref, hgs_ref,
                         sret_ref, shg_ref, b_ref, q_ref, *, l):
    t = pl.program_id(1)

    @pl.when(t == 0)
    def _():
        sret_ref[...] = jnp.zeros_like(sret_ref)
        shg_ref[...] = jnp.zeros_like(shg_ref)

    p = p_ref[...]
    prev = prev_ref[...]
    half = lax.broadcasted_iota(I32, (GQ_A * BLK, 1), 0) < BLK

    for kh in range(KVH_A):
        q = jnp.concatenate([p[:, C_QA + (kh * GQ_A + g) * HD_A:C_QA + (kh * GQ_A + g + 1) * HD_A]
                             for g in range(GQ_A)], axis=0)
        kband = jnp.concatenate([prev[:, kh * HD_A:(kh + 1) * HD_A],
                                 p[:, C_KA + kh * HD_A:C_KA + (kh + 1) * HD_A]], axis=0)
        vband = jnp.concatenate([prev[:, KV_A + kh * HD_A:KV_A + (kh + 1) * HD_A],
                                 p[:, C_VA + kh * HD_A:C_VA + (kh + 1) * HD_A]], axis=0)
        s = _mm_nt(q, kband) * (HD_A ** -0.5) + bias_ref[0, kh]
        sinkcol = jnp.where(half, scal_ref[l, kh * GQ_A], scal_ref[l, kh * GQ_A + 1])
        o = _softmax_sink_pv(s, sinkcol, vband)
        for g in range(GQ_A):
            c0 = (kh * GQ_A + g) * HD_A
            mix_ref[:, c0:c0 + HD_A] = o[g * BLK:(g + 1) * BLK]

    kd_t = (p[:, C_KB:C_KB + H_B * DK_B] * kdec_ref[...]).T
    qd = p[:, C_QB:C_QB + H_B * DK_B] * qdec_ref[...]
    for h in range(H_B):
        q = p[:, C_QB + h * DK_B:C_QB + (h + 1) * DK_B]
        k = p[:, C_KB + h * DK_B:C_KB + (h + 1) * DK_B] * (DK_B ** -0.5)
        v = p[:, C_VB + h * DV_B:C_VB + (h + 1) * DV_B]
        a = _mm_nt(q, k) * dmat_ref[h]
        s0 = sret_ref[h]
        o = _mm(a, v) + _mm(qd[:, h * DK_B:(h + 1) * DK_B], s0)
        sret_ref[h] = s0 * scal_ref[l, H_A + h] + _mm(kd_t[h * DK_B:(h + 1) * DK_B], v)
        c0 = H_A * HD_A + h * DV_B
        mix_ref[:, c0:c0 + DV_B] = _group_norm_gate(o, p[:, C_GB + h * DV_B:C_GB + (h + 1) * DV_B])

    o_diag, qe, k2, ebl = _hgrn_front(p, lb_ref[0], None, tril_ref[...], sel_ref[...], onesbd_ref[...],
                                      b_ref, q_ref, SUB_P)
    k2_t, ebl_t = k2.T, ebl.T
    v = p[:, C_IC:C_IC + HC]
    bdf = bdf_ref[...]
    st = shg_ref[...]
    o_inter = []
    for i in range(BLK // SUB_P):
        o_inter.append(_mm(qe[i * SUB_P:(i + 1) * SUB_P], st))
        u = _mm(k2_t * _col_mask(BLK, SUB_P, i), v) * bdf
        st = ebl_t[:, i * SUB_P:i * SUB_P + 1] * st + u
    shg_ref[...] = st
    o = jnp.concatenate(o_inter, axis=0) + o_diag
    c0 = H_A * HD_A + H_B * DV_B
    mix_ref[:, c0:c0 + HC] = _hgrn_finish(o, p, gain_ref[0], bdf)

    @pl.when(t == pl.num_programs(1) - 1)
    def _():
        kkeep_ref[0] = p[:, C_KA:C_KA + KV_A]
        vkeep_ref[0] = p[:, C_VA:C_VA + KV_A]
        rets_ref[0] = sret_ref[...]
        hgs_ref[0] = shg_ref[...]


def _mixer_prompt(proj, scal, lb, gain, l, bp, t_len, tabs, htab):
    nb = t_len // BLK
    const2 = lambda b, t: (0, 0)
    const3 = lambda b, t: (0, 0, 0)
    layer3 = lambda b, t: (l, 0, 0)
    full = lambda a: pl.BlockSpec(a.shape, const2 if a.ndim == 2 else const3)
    return pl.pallas_call(
        functools.partial(_mixer_prompt_kernel, l=l),
        grid=(bp, nb),
        in_specs=[pl.BlockSpec(memory_space=pltpu.SMEM),
                  pl.BlockSpec((BLK, D_PROJ), lambda b, t: (b * nb + t, 0)),
                  pl.BlockSpec((BLK, 2 * KV_A), lambda b, t: (b * nb + jnp.maximum(t - 1, 0), 1)),
                  pl.BlockSpec((1, KVH_A, GQ_A * BLK, 2 * BLK), lambda b, t: (jnp.minimum(t, 1), 0, 0, 0)),
                  full(tabs["dmat"]), full(tabs["qdec"]), full(tabs["kdec"]),
                  pl.BlockSpec((1, 1, HC), layer3), pl.BlockSpec((1, 1, HC), layer3),
                  full(htab["tril"]), full(htab["sel"]), full(htab["onesbd"]), full(htab["bdf"])],
        out_specs=[pl.BlockSpec((BLK, D_MODEL), lambda b, t: (b * nb + t, 0)),
                   pl.BlockSpec((1, WINDOW, KV_A), lambda b, t: (b, 0, 0)),
                   pl.BlockSpec((1, WINDOW, KV_A), lambda b, t: (b, 0, 0)),
                   pl.BlockSpec((1, H_B, DK_B, DV_B), lambda b, t: (b, 0, 0, 0)),
                   pl.BlockSpec((1, HC, HC), lambda b, t: (b, 0, 0))],
        out_shape=[jax.ShapeDtypeStruct((bp * t_len, D_MODEL), F32),
                   jax.ShapeDtypeStruct((bp, WINDOW, KV_A), F32),
                   jax.ShapeDtypeStruct((bp, WINDOW, KV_A), F32),
                   jax.ShapeDtypeStruct((bp, H_B, DK_B, DV_B), F32),
                   jax.ShapeDtypeStruct((bp, HC, HC), F32)],
        scratch_shapes=[pltpu.VMEM((H_B, DK_B, DV_B), F32), pltpu.VMEM((HC, HC), F32),
                        pltpu.VMEM((BLK, HC), F32), pltpu.VMEM((BLK, HC), F32)],
        compiler_params=_cparams("parallel", "arbitrary"),
        name="mixer_prompt",
    )(scal, proj, proj, tabs["bias"], tabs["dmat"], tabs["qdec"], tabs["kdec"], lb, gain,
      htab["tril"], htab["sel"], htab["onesbd"], htab["bdf"])


def _mixer_sample_kernel(scal_ref, p_ref, kbuf_ref, vbuf_ref, rs_ref, hs_ref, bias_ref, dblk_ref, qdec_ref,
                         kdec_ref, lb_ref, gain_ref, padmask_ref, tril_ref, sel_ref, onesbd_ref, bdf_ref,
                         mix_ref, rso_ref, hso_ref, kall_ref, vall_ref, oint_ref, b_ref, q_ref, *, l):
    p = p_ref[...]
    half = lax.broadcasted_iota(I32, (GQ_A * T_PAD, 1), 0) < T_PAD
    kall_ref[...] = jnp.zeros_like(kall_ref)
    vall_ref[...] = jnp.zeros_like(vall_ref)

    for bb in range(GB):
        r0 = bb * T_PAD
        kall_ref[0:WINDOW] = kbuf_ref[bb]
        vall_ref[0:WINDOW] = vbuf_ref[bb]
        kall_ref[WINDOW:WINDOW + T_PAD] = p[r0:r0 + T_PAD, C_KA:C_KA + KV_A]
        vall_ref[WINDOW:WINDOW + T_PAD] = p[r0:r0 + T_PAD, C_VA:C_VA + KV_A]
        kall, vall = kall_ref[...], vall_ref[...]
        for kh in range(KVH_A):
            q = jnp.concatenate([p[r0:r0 + T_PAD, C_QA + (kh * GQ_A + g) * HD_A:C_QA + (kh * GQ_A + g + 1) * HD_A]
                                 for g in range(GQ_A)], axis=0)
            s = _mm_nt(q, kall[:, kh * HD_A:(kh + 1) * HD_A]) * (HD_A ** -0.5) + bias_ref[kh]
            sinkcol = jnp.where(half, scal_ref[l, kh * GQ_A], scal_ref[l, kh * GQ_A + 1])
            o = _softmax_sink_pv(s, sinkcol, vall[:, kh * HD_A:(kh + 1) * HD_A])
            for g in range(GQ_A):
                c0 = (kh * GQ_A + g) * HD_A
                mix_ref[r0:r0 + T_PAD, c0:c0 + HD_A] = o[g * T_PAD:(g + 1) * T_PAD]

    kd_t = (p[:, C_KB:C_KB + H_B * DK_B] * kdec_ref[...]).T
    qd = p[:, C_QB:C_QB + H_B * DK_B] * qdec_ref[...]
    for h in range(H_B):
        q = p[:, C_QB + h * DK_B:C_QB + (h + 1) * DK_B]
        k = p[:, C_KB + h * DK_B:C_KB + (h + 1) * DK_B] * (DK_B ** -0.5)
        v = p[:, C_VB + h * DV_B:C_VB + (h + 1) * DV_B]
        a = _mm_nt(q, k) * dblk_ref[h]
        o_intra = _mm(a, v)
        for bb in range(GB):
            r0 = bb * T_PAD
            s0 = rs_ref[bb, h]
            oint_ref[r0:r0 + T_PAD, 0:DV_B] = _mm(qd[r0:r0 + T_PAD, h * DK_B:(h + 1) * DK_B], s0)
            rso_ref[bb, h] = (s0 * scal_ref[l, H_A + h]
                              + _mm(kd_t[h * DK_B:(h + 1) * DK_B] * _col_mask(BLK, T_PAD, bb), v))
        o = o_intra + oint_ref[:, 0:DV_B]
        c0 = H_A * HD_A + h * DV_B
        mix_ref[:, c0:c0 + DV_B] = _group_norm_gate(o, p[:, C_GB + h * DV_B:C_GB + (h + 1) * DV_B])

    o_diag, qe, k2, ebl = _hgrn_front(p, lb_ref[0], padmask_ref[...], tril_ref[...], sel_ref[...], onesbd_ref[...],
                                      b_ref, q_ref, T_PAD)
    k2_t, ebl_t = k2.T, ebl.T
    v = p[:, C_IC:C_IC + HC]
    for bb in range(GB):
        r0 = bb * T_PAD
        k2m = k2_t * _col_mask(BLK, T_PAD, bb)
        for h in range(H_C):
            s0 = hs_ref[bb, h]
            oint_ref[r0:r0 + T_PAD, h * DV_C:(h + 1) * DV_C] = _mm(qe[r0:r0 + T_PAD, h * DK_C:(h + 1) * DK_C], s0)
            hso_ref[bb, h] = (ebl_t[h * DK_C:(h + 1) * DK_C, r0:r0 + 1] * s0
                              + _mm(k2m[h * DK_C:(h + 1) * DK_C], v[:, h * DV_C:(h + 1) * DV_C]))
    o = oint_ref[...] + o_diag
    c0 = H_A * HD_A + H_B * DV_B
    mix_ref[:, c0:c0 + HC] = _hgrn_finish(o, p, gain_ref[0], bdf_ref[...])


def _mixer_sample(proj8, scal, kbuf, vbuf, rstate, hstate, lb, gain, l, bs, tabs, htab):
    nb = bs // GB
    full = lambda a: pl.BlockSpec(a.shape, (lambda i: (0, 0)) if a.ndim == 2 else (lambda i: (0, 0, 0)))
    layer3 = lambda i: (l, 0, 0)
    return pl.pallas_call(
        functools.partial(_mixer_sample_kernel, l=l),
        grid=(nb,),
        in_specs=[pl.BlockSpec(memory_space=pltpu.SMEM),
                  pl.BlockSpec((BLK, D_PROJ), lambda i: (i, 0)),
                  pl.BlockSpec((GB, WINDOW, KV_A), lambda i: (l * nb + i, 0, 0)),
                  pl.BlockSpec((GB, WINDOW, KV_A), lambda i: (l * nb + i, 0, 0)),
                  pl.BlockSpec((GB, H_B, DK_B, DV_B), lambda i: (l * nb + i, 0, 0, 0)),
                  pl.BlockSpec((GB, H_C, DK_C, DV_C), lambda i: (l * nb + i, 0, 0, 0)),
                  full(tabs["bias"]), full(tabs["dblk"]), full(tabs["qdec"]), full(tabs["kdec"]),
                  pl.BlockSpec((1, 1, HC), layer3), pl.BlockSpec((1, 1, HC), layer3), full(tabs["padmask"]),
                  full(htab["tril"]), full(htab["sel"]), full(htab["onesbd"]), full(htab["bdf"])],
        out_specs=[pl.BlockSpec((BLK, D_MODEL), lambda i: (i, 0)),
                   pl.BlockSpec((GB, H_B, DK_B, DV_B), lambda i: (i, 0, 0, 0)),
                   pl.BlockSpec((GB, H_C, DK_C, DV_C), lambda i: (i, 0, 0, 0))],
        out_shape=[jax.ShapeDtypeStruct((bs * T_PAD, D_MODEL), F32),
                   jax.ShapeDtypeStruct((bs, H_B, DK_B, DV_B), F32),
                   jax.ShapeDtypeStruct((bs, H_C, DK_C, DV_C), F32)],
        scratch_shapes=[pltpu.VMEM((2 * WINDOW, KV_A), F32), pltpu.VMEM((2 * WINDOW, KV_A), F32),
                        pltpu.VMEM((BLK, HC), F32), pltpu.VMEM((BLK, HC), F32), pltpu.VMEM((BLK, HC), F32)],
        compiler_params=_cparams("parallel"),
        name="mixer_sample",
    )(scal, proj8, kbuf, vbuf, rstate, hstate, tabs["bias"], tabs["dblk"], tabs["qdec"], tabs["kdec"],
      lb, gain, tabs["padmask"], htab["tril"], htab["sel"], htab["onesbd"], htab["bdf"])


def _attend_heads(q, mk_heads, mv_heads):
    outs = []
    for h in range(XH):
        s = _mm_nt(q[:, h * XHD:(h + 1) * XHD], mk_heads[h]) * (XHD ** -0.5)
        e = jnp.exp(s - jnp.max(s, axis=-1, keepdims=True))
        outs.append(_mm(e, mv_heads[h]) / jnp.sum(e, axis=-1, keepdims=True))
    return jnp.concatenate(outs, axis=-1)


def _attend_memory(q, mk, mv):
    return _attend_heads(q, [mk[:, h * XHD:(h + 1) * XHD] for h in range(XH)],
                         [mv[:, h * XHD:(h + 1) * XHD] for h in range(XH)])


def _xattn_prompt_kernel(x_ref, g_ref, wq_ref, wo_ref, mk_ref, mv_ref, o_ref):
    x = x_ref[...]
    q = jnp.dot(_rms(x, g_ref[0]).astype(BF16), wq_ref[0], preferred_element_type=F32)
    o = _attend_memory(q, mk_ref[...], mv_ref[...])
    o_ref[...] = x + jnp.dot(o.astype(BF16), wo_ref[0], preferred_element_type=F32)


def _xattn_prompt(x, g, wq, wo, mk, mv, l, bp, t_len):
    tq = _pick(t_len, (512, 256, 128))
    nt = t_len // tq
    dq = XH * XHD
    layer3 = lambda b, t: (l, 0, 0)
    return pl.pallas_call(
        _xattn_prompt_kernel,
        grid=(bp, nt),
        in_specs=[pl.BlockSpec((tq, D_MODEL), lambda b, t: (b * nt + t, 0)),
                  pl.BlockSpec((1, 1, D_MODEL), layer3),
                  pl.BlockSpec((1, D_MODEL, dq), layer3),
                  pl.BlockSpec((1, dq, D_MODEL), layer3),
                  pl.BlockSpec((N_MEM, dq), lambda b, t: (b, 0)),
                  pl.BlockSpec((N_MEM, dq), lambda b, t: (b, 0))],
        out_specs=pl.BlockSpec((tq, D_MODEL), lambda b, t: (b * nt + t, 0)),
        out_shape=jax.ShapeDtypeStruct(x.shape, F32),
        input_output_aliases={0: 0},
        compiler_params=_cparams("parallel", "parallel"),
        name="xattn_prompt",
    )(x, g, wq, wo, mk, mv)


XB = 4


def _xattn_sample_kernel(q_ref, mk_ref, mv_ref, o_ref):
    for bb in range(XB):
        r0 = bb * T_PAD
        o_ref[r0:r0 + T_PAD] = _attend_heads(q_ref[r0:r0 + T_PAD],
                                             [mk_ref[bb, :, h, :] for h in range(XH)],
                                             [mv_ref[bb, :, h, :] for h in range(XH)])


def _xattn_sample(q8, mk, mv, l, bs):
    dq = XH * XHD
    nb = bs // XB
    return pl.pallas_call(
        _xattn_sample_kernel,
        grid=(nb,),
        in_specs=[pl.BlockSpec((XB * T_PAD, dq), lambda i: (i, 0)),
                  pl.BlockSpec((XB, N_MEM, XH, XHD), lambda i: (l * nb + i, 0, 0, 0)),
                  pl.BlockSpec((XB, N_MEM, XH, XHD), lambda i: (l * nb + i, 0, 0, 0))],
        out_specs=pl.BlockSpec((XB * T_PAD, dq), lambda i: (i, 0)),
        out_shape=jax.ShapeDtypeStruct((bs * T_PAD, dq), F32),
        compiler_params=_cparams("parallel"),
        name="xattn_sample",
    )(q8, mk, mv)


def _ffn_kernel(be_ref, nused_ref, x_ref, g_ref, w1_ref, w3_ref, w2_ref, o_ref, xn_ref, *, dense):
    i, j = pl.program_id(0), pl.program_id(1)

    @pl.when(j == 0)
    def _():
        x = x_ref[...]
        if dense:
            xn_ref[...] = _rms(x, g_ref[0]).astype(BF16)
            o_ref[...] = x
        else:
            xn_ref[...] = x.astype(BF16)
            o_ref[...] = jnp.zeros_like(o_ref)

    @pl.when(i < nused_ref[0])
    def _():
        xn = xn_ref[...]
        h1 = jnp.dot(xn, w1_ref[0], preferred_element_type=F32)
        h3 = jnp.dot(xn, w3_ref[0], preferred_element_type=F32)
        o_ref[...] += jnp.dot((_silu(h1) * h3).astype(BF16), w2_ref[0], preferred_element_type=F32)


def _ffn(x, g, l, w1, w3, w2, block_e, nused, tm, dense):
    rows = x.shape[0]
    dff = w1.shape[2]
    tf = _pick(dff, (512, 256, 128))
    nf = dff // tf

    def xmap(i, j, be, nu):
        return (jnp.minimum(i, nu[0] - 1), 0)

    def w13map(i, j, be, nu):
        return (be[jnp.minimum(i, nu[0] - 1)], 0, jnp.where(i < nu[0], j, nf - 1))

    def w2map(i, j, be, nu):
        return (be[jnp.minimum(i, nu[0] - 1)], jnp.where(i < nu[0], j, nf - 1), 0)

    grid_spec = pltpu.PrefetchScalarGridSpec(
        num_scalar_prefetch=2,
        grid=(rows // tm, nf),
        in_specs=[pl.BlockSpec((tm, D_MODEL), xmap),
                  pl.BlockSpec((1, 1, D_MODEL), lambda i, j, be, nu: (l, 0, 0)),
                  pl.BlockSpec((1, D_MODEL, tf), w13map),
                  pl.BlockSpec((1, D_MODEL, tf), w13map),
                  pl.BlockSpec((1, tf, D_MODEL), w2map)],
        out_specs=pl.BlockSpec((tm, D_MODEL), lambda i, j, be, nu: (i, 0)),
        scratch_shapes=[pltpu.VMEM((tm, D_MODEL), BF16)],
    )
    return pl.pallas_call(
        functools.partial(_ffn_kernel, dense=dense),
        grid_spec=grid_spec,
        out_shape=jax.ShapeDtypeStruct((rows, D_MODEL), F32),
        input_output_aliases={2: 0} if dense else {},
        compiler_params=_cparams("arbitrary", "arbitrary"),
        name="ffn_dense" if dense else "ffn_experts",
    )(block_e, nused, x, g, w1, w3, w2)


def _router_kernel(x_ref, g_ref, wr_ref, h_ref, idx_ref, gate_ref):
    h = _rms(x_ref[...], g_ref[0])
    h_ref[...] = h
    logits = _mm_hi(h, wr_ref[...])
    lane = lax.broadcasted_iota(I32, logits.shape, 1)
    logits = jnp.where(lane < N_EXPERTS, logits, NEG_INF)
    m1 = jnp.max(logits, axis=-1, keepdims=True)
    i1 = jnp.min(jnp.where(logits == m1, lane, LANES), axis=-1, keepdims=True)
    rest = jnp.where(lane == i1, NEG_INF, logits)
    m2 = jnp.max(rest, axis=-1, keepdims=True)
    i2 = jnp.min(jnp.where(rest == m2, lane, LANES), axis=-1, keepdims=True)
    e2 = jnp.exp(m2 - m1)
    den = 1.0 + e2
    idx_ref[...] = jnp.where(lane == 0, i1, jnp.where(lane == 1, i2, 0))
    gate_ref[...] = jnp.where(lane == 0, 1.0 / den, jnp.where(lane == 1, e2 / den, 0.0))


def _router(x, g, l, w_router):
    n = x.shape[0]
    tm = _pick(n, ROW_TILES)
    wr = jnp.pad(w_router, ((0, 0), (0, LANES - N_EXPERTS)))
    return pl.pallas_call(
        _router_kernel,
        grid=(n // tm,),
        in_specs=[pl.BlockSpec((tm, D_MODEL), lambda i: (i, 0)),
                  pl.BlockSpec((1, 1, D_MODEL), lambda i: (l, 0, 0)),
                  pl.BlockSpec((D_MODEL, LANES), lambda i: (0, 0))],
        out_specs=[pl.BlockSpec((tm, D_MODEL), lambda i: (i, 0)),
                   pl.BlockSpec((tm, LANES), lambda i: (i, 0)),
                   pl.BlockSpec((tm, LANES), lambda i: (i, 0))],
        out_shape=[jax.ShapeDtypeStruct((n, D_MODEL), F32),
                   jax.ShapeDtypeStruct((n, LANES), I32),
                   jax.ShapeDtypeStruct((n, LANES), F32)],
        compiler_params=_cparams("parallel"),
        name="router",
    )(x, g, wr)


GATHER_UNROLL = 8


def _gather_rows_kernel(tok_ref, src_ref, o_ref, sem, *, tg):
    base = pl.program_id(0) * tg

    def issue(r, c):
        pltpu.make_async_copy(src_ref.at[pl.ds(tok_ref[base + r], 1)], o_ref.at[pl.ds(r, 1)], sem).start()
        return c

    lax.fori_loop(0, tg, issue, 0, unroll=GATHER_UNROLL)
    pltpu.make_async_copy(src_ref.at[pl.ds(0, tg)], o_ref, sem).wait()


def _gather_rows(src, tok, tg):
    rows = tok.shape[0]
    grid_spec = pltpu.PrefetchScalarGridSpec(
        num_scalar_prefetch=1,
        grid=(rows // tg,),
        in_specs=[pl.BlockSpec(memory_space=pl.ANY)],
        out_specs=pl.BlockSpec((tg, D_MODEL), lambda i, tok: (i, 0)),
        scratch_shapes=[pltpu.SemaphoreType.DMA],
    )
    return pl.pallas_call(
        functools.partial(_gather_rows_kernel, tg=tg),
        grid_spec=grid_spec,
        out_shape=jax.ShapeDtypeStruct((rows, D_MODEL), F32),
        compiler_params=_cparams("arbitrary"),
        name="moe_gather",
    )(tok, src)


def _combine_kernel(pos_ref, x_ref, gate_ref, y_ref, o_ref, buf_ref, sem, *, tc):
    base = pl.program_id(0) * tc

    def issue(r, c):
        for k in range(TOP_K):
            pltpu.make_async_copy(y_ref.at[pl.ds(pos_ref[(base + r) * TOP_K + k], 1)],
                                  buf_ref.at[k, pl.ds(r, 1)], sem).start()
        return c

    lax.fori_loop(0, tc, issue, 0, unroll=GATHER_UNROLL)
    for k in range(TOP_K):
        pltpu.make_async_copy(y_ref.at[pl.ds(0, tc)], buf_ref.at[k], sem).wait()
    gate = gate_ref[...]
    o_ref[...] = x_ref[...] + (buf_ref[0] * gate[:, 0:1] + buf_ref[1] * gate[:, 1:2])


def _combine(x, gate, y, pos, tc):
    n = x.shape[0]
    grid_spec = pltpu.PrefetchScalarGridSpec(
        num_scalar_prefetch=1,
        grid=(n // tc,),
        in_specs=[pl.BlockSpec((tc, D_MODEL), lambda i, pos: (i, 0)),
                  pl.BlockSpec((tc, LANES), lambda i, pos: (i, 0)),
                  pl.BlockSpec(memory_space=pl.ANY)],
        out_specs=pl.BlockSpec((tc, D_MODEL), lambda i, pos: (i, 0)),
        scratch_shapes=[pltpu.VMEM((TOP_K, tc, D_MODEL), F32), pltpu.SemaphoreType.DMA],
    )
    return pl.pallas_call(
        functools.partial(_combine_kernel, tc=tc),
        grid_spec=grid_spec,
        out_shape=jax.ShapeDtypeStruct(x.shape, F32),
        input_output_aliases={1: 0},
        compiler_params=_cparams("arbitrary"),
        name="moe_combine",
    )(pos, x, gate, y)


def _moe(x, g, l, w_router, w1, w3, w2, e0):
    n = x.shape[0]
    tm = _pick(n, ROW_TILES)
    h, idx, gate = _router(x, g, l, w_router)
    a = n * TOP_K
    e_flat = idx[:, :TOP_K].reshape(a)
    onehot = (e_flat[:, None] == jnp.arange(N_EXPERTS, dtype=I32)[None, :]).astype(I32)
    csum = jnp.cumsum(onehot, axis=0)
    counts = csum[-1]
    rank = jnp.sum((csum - onehot) * onehot, axis=1)
    padded = (counts + tm - 1) // tm * tm
    pend = jnp.cumsum(padded)
    pstart = pend - padded
    dest = (pstart[e_flat] + rank).astype(I32)
    nb = a // tm + N_EXPERTS
    tok = jnp.zeros((nb * tm,), I32).at[dest].set(jnp.repeat(jnp.arange(n, dtype=I32), TOP_K))
    block_start = jnp.arange(nb, dtype=I32) * tm
    block_e = jnp.minimum(jnp.sum((pend[None, :] <= block_start[:, None]).astype(I32), axis=1), N_EXPERTS - 1)
    nused = (pend[-1:] // tm).astype(I32)
    xg = _gather_rows(h, tok, tm)
    y = _ffn(xg, g, l, w1, w3, w2, (block_e + e0).astype(I32), nused, tm, dense=False)
    return _combine(x, gate, y, dest, _pick(n, (256, 128, 64, 32, 16, 8)))


def kernel(x_prompt, x_sample, cache_swa_k, cache_swa_v, state_ret, state_hgrn, cache_mem_k, cache_mem_v, mem_prompt, norm_mix, w_in, swa_sinks, hgrn_lb_logits, hgrn_norm, w_out, norm_xattn, norm_mem, wx_q, wx_k, wx_v, wx_o, norm_ffn, ffn_w1, ffn_w3, ffn_w2, moe_router, moe_w1, moe_w3, moe_w2, final_norm):
    bp, t_len, _ = x_prompt.shape
    bs, ts, _ = x_sample.shape
    depth = w_in.shape[0]
    n_p, n_s = bp * t_len, bs * ts
    n = n_p + n_s
    assert t_len % BLK == 0 and bs % GB == 0 and bs % XB == 0 and ts <= T_PAD
    assert cache_swa_k.shape[2] == WINDOW and ts < WINDOW
    dq = XH * XHD

    x = jnp.concatenate([x_prompt.reshape(n_p, D_MODEL), x_sample.reshape(n_s, D_MODEL)], axis=0)
    ptab, stab = _prompt_tables(), _sample_tables(ts)
    htab_p, htab_s = _hgrn_tables(SUB_P), _hgrn_tables(T_PAD)
    lb_p = jax.nn.softmax(hgrn_lb_logits.astype(F32), axis=0)
    lower_bounds = (jnp.cumsum(lb_p, axis=0) - lb_p[0]).reshape(depth, 1, HC)
    gains = hgrn_norm.astype(F32).reshape(depth, 1, HC)
    sinks = swa_sinks.astype(F32)
    scal_p = jnp.concatenate([sinks, jnp.broadcast_to(jnp.asarray(ptab["cdec"], F32), (depth, H_B))], axis=1)
    scal_s = jnp.concatenate([sinks, jnp.broadcast_to(jnp.asarray(stab["cdec"], F32), (depth, H_B))], axis=1)
    mem = mem_prompt.reshape(bp * N_MEM, D_MODEL)

    bf = lambda w: w.astype(BF16)
    w_in_b, w_out_b = bf(w_in), bf(w_out)
    wq_b, wk_b, wv_b, wo_b = bf(wx_q), bf(wx_k), bf(wx_v), bf(wx_o)
    f1_b, f3_b, f2_b = bf(ffn_w1), bf(ffn_w3), bf(ffn_w2)
    m1_b = bf(moe_w1.reshape((-1,) + moe_w1.shape[2:]))
    m3_b = bf(moe_w3.reshape((-1,) + moe_w3.shape[2:]))
    m2_b = bf(moe_w2.reshape((-1,) + moe_w2.shape[2:]))
    vec3 = lambda v: v.astype(F32).reshape(depth, 1, D_MODEL)
    n_mix, n_xat, n_mem, n_ffn = vec3(norm_mix), vec3(norm_xattn), vec3(norm_mem), vec3(norm_ffn)

    kbuf = cache_swa_k.astype(F32).reshape(depth * bs, WINDOW, KV_A)
    vbuf = cache_swa_v.astype(F32).reshape(depth * bs, WINDOW, KV_A)
    rstate = state_ret.astype(F32).reshape((depth * bs,) + state_ret.shape[2:])
    hstate = state_hgrn.astype(F32).reshape((depth * bs,) + state_hgrn.shape[2:])
    cmk = cache_mem_k.astype(F32).reshape(depth * bs, N_MEM, XH, XHD)
    cmv = cache_mem_v.astype(F32).reshape(depth * bs, N_MEM, XH, XHD)

    def pad_tokens(a):
        return jnp.pad(a.reshape(bs, ts, -1), ((0, 0), (0, T_PAD - ts), (0, 0))).reshape(bs * T_PAD, -1)

    def unpad_tokens(a):
        return a.reshape(bs, T_PAD, -1)[:, :ts].reshape(bs * ts, -1)

    outs = {k: [] for k in ("pk", "pv", "pr", "pc", "pmk", "pmv", "sk", "sv", "sr", "sc")}
    for l in range(depth):
        proj = _rms_matmul(x, n_mix, w_in_b, l)
        mix_p, kkeep, vkeep, ret_p, hg_p = _mixer_prompt(proj, scal_p, lower_bounds, gains, l, bp, t_len, ptab, htab_p)
        proj_s = lax.slice_in_dim(proj, n_p, n, axis=0)
        mix_s, ret_s, hg_s = _mixer_sample(pad_tokens(proj_s), scal_s, kbuf, vbuf, rstate, hstate,
                                           lower_bounds, gains, l, bs, stab, htab_s)
        x = _matmul_res(mix_p, w_out_b, l, x, 0)
        x = _matmul_res(unpad_tokens(mix_s), w_out_b, l, x, n_p)

        outs["pk"].append(kkeep.reshape(bp, WINDOW, KVH_A, HD_A))
        outs["pv"].append(vkeep.reshape(bp, WINDOW, KVH_A, HD_A))
        outs["pr"].append(ret_p)
        outs["pc"].append(jnp.stack([hg_p[:, h * DK_C:(h + 1) * DK_C, h * DV_C:(h + 1) * DV_C]
                                     for h in range(H_C)], axis=1))
        new_k = proj_s[:, C_KA:C_KA + KV_A].reshape(bs, ts, KVH_A, HD_A)
        new_v = proj_s[:, C_VA:C_VA + KV_A].reshape(bs, ts, KVH_A, HD_A)
        outs["sk"].append(jnp.concatenate([cache_swa_k[l][:, ts:].astype(F32), new_k], axis=1))
        outs["sv"].append(jnp.concatenate([cache_swa_v[l][:, ts:].astype(F32), new_v], axis=1))
        outs["sr"].append(ret_s)
        outs["sc"].append(hg_s)

        mk = _rms_matmul(mem, n_mem, wk_b, l)
        mv = _rms_matmul(mem, n_mem, wv_b, l)
        outs["pmk"].append(mk.reshape(bp, N_MEM, XH, XHD))
        outs["pmv"].append(mv.reshape(bp, N_MEM, XH, XHD))
        x = _xattn_prompt(x, n_xat, wq_b, wo_b, mk, mv, l, bp, t_len)
        q_s = _rms_matmul(x, n_xat, wq_b, l, row_off=n_p, rows=n_s)
        o_s = _xattn_sample(pad_tokens(q_s), cmk, cmv, l, bs)
        x = _matmul_res(unpad_tokens(o_s), wo_b, l, x, n_p)

        e = l // 2
        if l % 2 == 0:
            tm = _pick(n, (1536, 768) + ROW_TILES)
            nb = n // tm
            x = _ffn(x, n_ffn, l, f1_b, f3_b, f2_b, jnp.full((nb,), e, I32), jnp.full((1,), nb, I32), tm, dense=True)
        else:
            x = _moe(x, n_ffn, l, moe_router[e], m1_b, m3_b, m2_b, e * N_EXPERTS)

    y_p = _rmsnorm_rows(x, final_norm, 0, n_p).reshape(bp, t_len, D_MODEL)
    y_s = _rmsnorm_rows(x, final_norm, n_p, n_s).reshape(bs, ts, D_MODEL)
    st = lambda k: jnp.stack(outs[k])
    return (y_p, y_s, st("pk"), st("pv"), st("pr"), st("pc"), st("pmk"), st("pmv"),
            st("sk"), st("sv"), st("sr"), st("sc"))
```

```python
import functools

import numpy as np
import jax
import jax.numpy as jnp
from jax import lax
from jax.experimental import pallas as pl
from jax.experimental.pallas import tpu as pltpu

F32, BF16, I32 = jnp.float32, jnp.bfloat16, jnp.int32
HIGHEST = lax.Precision.HIGHEST
NEG_INF = float("-inf")

D_MODEL = 1024
WINDOW = 128
H_A, KVH_A, GQ_A, HD_A = 4, 2, 2, 64
ALIBI_MAX_EXP = 8.0
H_B, DK_B, DV_B = 4, 64, 128
RET_CHUNK = 128
H_C, DK_C, DV_C = 4, 64, 64
N_MEM, XH, XHD = 256, 4, 128
N_EXPERTS, TOP_K = 8, 2
EPS = 1e-6
D_PROJ = 3072
C_QA, C_KA, C_VA = 0, 256, 384
C_QB, C_KB, C_VB, C_GB = 512, 768, 1024, 1536
C_FC, C_QC, C_IC, C_GC = 2048, 2304, 2560, 2816
KV_A = KVH_A * HD_A
HC = H_C * DK_C

BLK = 128
SUB_P = 16
T_PAD = 8
GB = BLK // T_PAD
LANES = 128
VMEM_LIMIT = 56 * 1024 * 1024
ROW_TILES = (512, 256, 128, 64, 32, 16, 8)


def _pick(n, cands):
    for c in cands:
        if n % c == 0:
            return c
    raise ValueError(f"no tile for {n}")


def _row_tile(rows, row_off):
    return _pick(int(np.gcd(rows, row_off)) if row_off else rows, ROW_TILES)


def _cparams(*sem):
    return pltpu.CompilerParams(dimension_semantics=sem, vmem_limit_bytes=VMEM_LIMIT)


def _mm(a, b):
    return jnp.dot(a.astype(BF16), b.astype(BF16), preferred_element_type=F32)


def _mm_nt(a, b):
    return lax.dot_general(a.astype(BF16), b.astype(BF16), (((1,), (1,)), ((), ())),
                           preferred_element_type=F32)


def _mm_hi(a, b):
    return jnp.dot(a, b, precision=HIGHEST, preferred_element_type=F32)


def _sigmoid(x):
    return 1.0 / (1.0 + jnp.exp(-x))


def _silu(x):
    return x * _sigmoid(x)


def _rms(x, g):
    return x * lax.rsqrt(jnp.mean(x * x, axis=-1, keepdims=True) + EPS) * g


def _rms_matmul_kernel(x_ref, g_ref, w_ref, o_ref, xn_ref):
    @pl.when(pl.program_id(1) == 0)
    def _():
        xn_ref[...] = _rms(x_ref[...], g_ref[0]).astype(BF16)

    o_ref[...] = jnp.dot(xn_ref[...], w_ref[0], preferred_element_type=F32)


def _rms_matmul(x, g, w, l, row_off=0, rows=None):
    n, k = x.shape
    m = w.shape[2]
    rows = n if rows is None else rows
    tm = _row_tile(rows, row_off)
    tn = _pick(m, (3072, 1536, 1024, 512, 256, 128))
    off = row_off // tm
    return pl.pallas_call(
        _rms_matmul_kernel,
        grid=(rows // tm, m // tn),
        in_specs=[pl.BlockSpec((tm, k), lambda i, j: (i + off, 0)),
                  pl.BlockSpec((1, 1, k), lambda i, j: (l, 0, 0)),
                  pl.BlockSpec((1, k, tn), lambda i, j: (l, 0, j))],
        out_specs=pl.BlockSpec((tm, tn), lambda i, j: (i, j)),
        out_shape=jax.ShapeDtypeStruct((rows, m), F32),
        scratch_shapes=[pltpu.VMEM((tm, k), BF16)],
        compiler_params=_cparams("parallel", "arbitrary"),
        name="rms_matmul",
    )(x, g, w)


def _matmul_res_kernel(a_ref, w_ref, x_ref, o_ref):
    o_ref[...] = x_ref[...] + jnp.dot(a_ref[...].astype(BF16), w_ref[0], preferred_element_type=F32)


def _matmul_res(a, w, l, x, row_off):
    rows, k = a.shape
    m = w.shape[2]
    tm = _row_tile(rows, row_off)
    off = row_off // tm
    return pl.pallas_call(
        _matmul_res_kernel,
        grid=(rows // tm,),
        in_specs=[pl.BlockSpec((tm, k), lambda i: (i, 0)),
                  pl.BlockSpec((1, k, m), lambda i: (l, 0, 0)),
                  pl.BlockSpec((tm, m), lambda i: (i + off, 0))],
        out_specs=pl.BlockSpec((tm, m), lambda i: (i + off, 0)),
        out_shape=jax.ShapeDtypeStruct(x.shape, F32),
        input_output_aliases={2: 0},
        compiler_params=_cparams("parallel"),
        name="matmul_res",
    )(a, w, x)


def _rmsnorm_kernel(x_ref, g_ref, o_ref):
    o_ref[...] = _rms(x_ref[...], g_ref[...])


def _rmsnorm_rows(x, g, row_off, rows):
    n, k = x.shape
    tm = _row_tile(rows, row_off)
    off = row_off // tm
    return pl.pallas_call(
        _rmsnorm_kernel,
        grid=(rows // tm,),
        in_specs=[pl.BlockSpec((tm, k), lambda i: (i + off, 0)),
                  pl.BlockSpec((1, k), lambda i: (0, 0))],
        out_specs=pl.BlockSpec((tm, k), lambda i: (i, 0)),
        out_shape=jax.ShapeDtypeStruct((rows, k), F32),
        compiler_params=_cparams("parallel"),
        name="final_rmsnorm",
    )(x, g.reshape(1, k))


def _retention_gamma():
    return 1.0 - np.exp2(-5.0 - np.arange(H_B, dtype=np.float64))


def _alibi_slopes():
    return np.exp2(-(ALIBI_MAX_EXP / H_A) * np.arange(1, H_A + 1, dtype=np.float64))


def _head_cols(vals_hr, width):
    return np.repeat(vals_hr.T[:, :, None], width, axis=2).reshape(vals_hr.shape[1], -1)


def _hgrn_tables(sub):
    r = np.arange(BLK)
    same = (r[:, None] // sub) == (r[None, :] // sub)
    tril = (same & (r[None, :] <= r[:, None])).astype(np.float32)
    pr = np.arange(BLK * sub)
    sel = (r[:, None] == (pr // sub)[None, :]).astype(np.float32)
    hh = np.arange(HC) // DK_C
    bd = (hh[:, None] == hh[None, :]).astype(np.float32)
    return dict(tril=jnp.asarray(tril), sel=jnp.asarray(sel, dtype=BF16),
                onesbd=jnp.asarray(bd, dtype=BF16), bdf=jnp.asarray(bd))


def _prompt_tables():
    gam = _retention_gamma()
    i = np.arange(RET_CHUNK, dtype=np.float64)
    diff = i[:, None] - i[None, :]
    dmat = np.where(diff >= 0, gam[:, None, None] ** np.maximum(diff, 0.0), 0.0)
    qdec = _head_cols(gam[:, None] ** (i + 1.0), DK_B)
    kdec = _head_cols(gam[:, None] ** (RET_CHUNK - 1.0 - i), DK_B) * (DK_B ** -0.5)
    cdec = gam ** RET_CHUNK
    slopes = _alibi_slopes().reshape(KVH_A, GQ_A)
    qi = np.tile(np.arange(WINDOW), GQ_A)[:, None]
    qg = np.repeat(np.arange(GQ_A), WINDOW)[:, None]
    j = np.arange(2 * WINDOW)[None, :]
    rel = WINDOW + qi - j
    ok = (rel >= 0) & (rel < WINDOW)
    bias = np.stack([np.where(ok, -slopes[kh][qg] * rel, NEG_INF) for kh in range(KVH_A)])
    bias_first = np.where(j < WINDOW, NEG_INF, bias)
    return dict(dmat=jnp.asarray(dmat, dtype=F32), qdec=jnp.asarray(qdec, dtype=F32),
                kdec=jnp.asarray(kdec, dtype=F32), cdec=cdec,
                bias=jnp.asarray(np.stack([bias_first, bias]), dtype=F32))


def _sample_tables(t):
    gam = _retention_gamma()
    r = np.arange(BLK)
    ti, bi = r % T_PAD, r // T_PAD
    diff = (ti[:, None] - ti[None, :]).astype(np.float64)
    same = bi[:, None] == bi[None, :]
    dblk = np.where(same & (diff >= 0), gam[:, None, None] ** np.maximum(diff, 0.0), 0.0)
    qdec = _head_cols(gam[:, None] ** (ti + 1.0), DK_B)
    kdec = _head_cols(gam[:, None] ** (t - 1.0 - ti), DK_B) * (DK_B ** -0.5)
    cdec = gam ** t
    slopes = _alibi_slopes().reshape(KVH_A, GQ_A)
    qt = np.tile(np.arange(T_PAD), GQ_A)[:, None]
    qg = np.repeat(np.arange(GQ_A), T_PAD)[:, None]
    c = np.arange(2 * WINDOW)[None, :]
    rel = qt + WINDOW - c
    ok = (rel >= 0) & (rel < WINDOW) & (c < WINDOW + t)
    bias = np.stack([np.where(ok, -slopes[kh][qg] * rel, NEG_INF) for kh in range(KVH_A)])
    padmask = np.repeat((ti < t).astype(np.float32)[:, None], HC, axis=1)
    return dict(dblk=jnp.asarray(dblk, dtype=F32), qdec=jnp.asarray(qdec, dtype=F32),
                kdec=jnp.asarray(kdec, dtype=F32), cdec=cdec, bias=jnp.asarray(bias, dtype=F32),
                padmask=jnp.asarray(padmask))


def _tile_j(x, sub):
    nsub = x.shape[0] // sub
    y = jnp.broadcast_to(x.reshape(nsub, 1, sub, x.shape[1]), (nsub, sub, sub, x.shape[1]))
    return y.reshape(nsub * sub * sub, x.shape[1])


def _row_bcast(ref, r, sub):
    return jnp.broadcast_to(ref[r:r + 1, :], (sub, ref.shape[1]))


def _hgrn_front(p, lb, padmask, tril, sel, onesbd, b_ref, q_ref, sub):
    fc, qc, ic = p[:, C_FC:C_FC + HC], p[:, C_QC:C_QC + HC], p[:, C_IC:C_IC + HC]
    f = lb + (1.0 - lb) * _sigmoid(fc)
    kk = 1.0 - f
    g = jnp.log(f)
    if padmask is not None:
        kk = kk * padmask
        g = g * padmask
    qq = _silu(qc)
    b = _mm_hi(tril, g)
    b_ref[...] = b
    q_ref[...] = qq
    jrow = lax.broadcasted_iota(I32, (sub, 1), 0)
    bi = jnp.concatenate([_row_bcast(b_ref, r, sub) for r in range(BLK)], axis=0)
    qi = jnp.concatenate([jnp.where(jrow <= r % sub, _row_bcast(q_ref, r, sub), 0.0) for r in range(BLK)], axis=0)
    w = jnp.exp(jnp.minimum(bi - _tile_j(b, sub), 0.0))
    wm = qi * _tile_j(kk, sub) * w
    ab = jnp.dot(wm.astype(BF16), onesbd, preferred_element_type=F32)
    cm = ab * _tile_j(ic, sub)
    o_diag = jnp.dot(sel, cm.astype(BF16), preferred_element_type=F32)
    blrep = jnp.concatenate([_row_bcast(b_ref, i * sub + sub - 1, sub) for i in range(BLK // sub)], axis=0)
    return o_diag, qq * jnp.exp(b), kk * jnp.exp(blrep - b), jnp.exp(blrep)


def _hgrn_finish(o, p, gain, bdf):
    ms = _mm_hi(o * o, bdf) * (1.0 / DV_C)
    return o * lax.rsqrt(ms + EPS) * gain * _silu(p[:, C_GC:C_GC + HC])


def _group_norm_gate(o, gate):
    c = o - jnp.mean(o, axis=-1, keepdims=True)
    return c * lax.rsqrt(jnp.mean(c * c, axis=-1, keepdims=True) + EPS) * _silu(gate)


def _softmax_sink_pv(s, sinkcol, v):
    m = jnp.maximum(jnp.max(s, axis=-1, keepdims=True), sinkcol)
    e = jnp.exp(s - m)
    den = jnp.sum(e, axis=-1, keepdims=True) + jnp.exp(sinkcol - m)
    return _mm(e, v) / den


def _col_mask(width, sub, idx):
    return (lax.broadcasted_iota(I32, (1, width), 1) // sub == idx).astype(F32)


def _mixer_prompt_kernel(scal_ref, p_ref, prev_ref, bias_ref, dmat_ref, qdec_ref, kdec_ref, lb_ref, gain_ref,
                         tril_ref, sel_ref, onesbd_ref, bdf_ref,
                         mix_ref, kkeep_ref, vkeep_ref, rets_ref, hgs_ref,
                         sret_ref, shg_ref, b_ref, q_ref, *, l):
    t = pl.program_id(1)

    @pl.when(t == 0)
    def _():
        sret_ref[...] = jnp.zeros_like(sret_ref)
        shg_ref[...] = jnp.zeros_like(shg_ref)

    p = p_ref[...]
    prev = prev_ref[...]
    half = lax.broadcasted_iota(I32, (GQ_A * BLK, 1), 0) < BLK

    for kh in range(KVH_A):
        q = jnp.concatenate([p[:, C_QA + (kh * GQ_A + g) * HD_A:C_QA + (kh * GQ_A + g + 1) * HD_A]
                             for g in range(GQ_A)], axis=0)
        kband = jnp.concatenate([prev[:, kh * HD_A:(kh + 1) * HD_A],
                                 p[:, C_KA + kh * HD_A:C_KA + (kh + 1) * HD_A]], axis=0)
        vband = jnp.concatenate([prev[:, KV_A + kh * HD_A:KV_A + (kh + 1) * HD_A],
                                 p[:, C_VA + kh * HD_A:C_VA + (kh + 1) * HD_A]], axis=0)
        s = _mm_nt(q, kband) * (HD_A ** -0.5) + bias_ref[0, kh]
        sinkcol = jnp.where(half, scal_ref[l, kh * GQ_A], scal_ref[l, kh * GQ_A + 1])
        o = _softmax_sink_pv(s, sinkcol, vband)
        for g in range(GQ_A):
            c0 = (kh * GQ_A + g) * HD_A
            mix_ref[:, c0:c0 + HD_A] = o[g * BLK:(g + 1) * BLK]

    kd_t = (p[:, C_KB:C_KB + H_B * DK_B] * kdec_ref[...]).T
    qd = p[:, C_QB:C_QB + H_B * DK_B] * qdec_ref[...]
    for h in range(H_B):
        q = p[:, C_QB + h * DK_B:C_QB + (h + 1) * DK_B]
        k = p[:, C_KB + h * DK_B:C_KB + (h + 1) * DK_B] * (DK_B ** -0.5)
        v = p[:, C_VB + h * DV_B:C_VB + (h + 1) * DV_B]
        a = _mm_nt(q, k) * dmat_ref[h]
        s0 = sret_ref[h]
        o = _mm(a, v) + _mm(qd[:, h * DK_B:(h + 1) * DK_B], s0)
        sret_ref[h] = s0 * scal_ref[l, H_A + h] + _mm(kd_t[h * DK_B:(h + 1) * DK_B], v)
        c0 = H_A * HD_A + h * DV_B
        mix_ref[:, c0:c0 + DV_B] = _group_norm_gate(o, p[:, C_GB + h * DV_B:C_GB + (h + 1) * DV_B])

    o_diag, qe, k2, ebl = _hgrn_front(p, lb_ref[0], None, tril_ref[...], sel_ref[...], onesbd_ref[...],
                                      b_ref, q_ref, SUB_P)
    k2_t, ebl_t = k2.T, ebl.T
    v = p[:, C_IC:C_IC + HC]
    bdf = bdf_ref[...]
    st = shg_ref[...]
    o_inter = []
    for i in range(BLK // SUB_P):
        o_inter.append(_mm(qe[i * SUB_P:(i + 1) * SUB_P], st))
        u = _mm(k2_t * _col_mask(BLK, SUB_P, i), v) * bdf
        st = ebl_t[:, i * SUB_P:i * SUB_P + 1] * st + u
    shg_ref[...] = st
    o = jnp.concatenate(o_inter, axis=0) + o_diag
    c0 = H_A * HD_A + H_B * DV_B
    mix_ref[:, c0:c0 + HC] = _hgrn_finish(o, p, gain_ref[0], bdf)

    @pl.when(t == pl.num_programs(1) - 1)
    def _():
        kkeep_ref[0] = p[:, C_KA:C_KA + KV_A]
        vkeep_ref[0] = p[:, C_VA:C_VA + KV_A]
        rets_ref[0] = sret_ref[...]
        hgs_ref[0] = shg_ref[...]


def _mixer_prompt(proj, scal, lb, gain, l, bp, t_len, tabs, htab):
    nb = t_len // BLK
    const2 = lambda b, t: (0, 0)
    const3 = lambda b, t: (0, 0, 0)
    layer3 = lambda b, t: (l, 0, 0)
    full = lambda a: pl.BlockSpec(a.shape, const2 if a.ndim == 2 else const3)
    return pl.pallas_call(
        functools.partial(_mixer_prompt_kernel, l=l),
        grid=(bp, nb),
        in_specs=[pl.BlockSpec(memory_space=pltpu.SMEM),
                  pl.BlockSpec((BLK, D_PROJ), lambda b, t: (b * nb + t, 0)),
                  pl.BlockSpec((BLK, 2 * KV_A), lambda b, t: (b * nb + jnp.maximum(t - 1, 0), 1)),
                  pl.BlockSpec((1, KVH_A, GQ_A * BLK, 2 * BLK), lambda b, t: (jnp.minimum(t, 1), 0, 0, 0)),
                  full(tabs["dmat"]), full(tabs["qdec"]), full(tabs["kdec"]),
                  pl.BlockSpec((1, 1, HC), layer3), pl.BlockSpec((1, 1, HC), layer3),
                  full(htab["tril"]), full(htab["sel"]), full(htab["onesbd"]), full(htab["bdf"])],
        out_specs=[pl.BlockSpec((BLK, D_MODEL), lambda b, t: (b * nb + t, 0)),
                   pl.BlockSpec((1, WINDOW, KV_A), lambda b, t: (b, 0, 0)),
                   pl.BlockSpec((1, WINDOW, KV_A), lambda b, t: (b, 0, 0)),
                   pl.BlockSpec((1, H_B, DK_B, DV_B), lambda b, t: (b, 0, 0, 0)),
                   pl.BlockSpec((1, HC, HC), lambda b, t: (b, 0, 0))],
        out_shape=[jax.ShapeDtypeStruct((bp * t_len, D_MODEL), F32),
                   jax.ShapeDtypeStruct((bp, WINDOW, KV_A), F32),
                   jax.ShapeDtypeStruct((bp, WINDOW, KV_A), F32),
                   jax.ShapeDtypeStruct((bp, H_B, DK_B, DV_B), F32),
                   jax.ShapeDtypeStruct((bp, HC, HC), F32)],
        scratch_shapes=[pltpu.VMEM((H_B, DK_B, DV_B), F32), pltpu.VMEM((HC, HC), F32),
                        pltpu.VMEM((BLK, HC), F32), pltpu.VMEM((BLK, HC), F32)],
        compiler_params=_cparams("parallel", "arbitrary"),
        name="mixer_prompt",
    )(scal, proj, proj, tabs["bias"], tabs["dmat"], tabs["qdec"], tabs["kdec"], lb, gain,
      htab["tril"], htab["sel"], htab["onesbd"], htab["bdf"])


def _mixer_sample_kernel(scal_ref, p_ref, kbuf_ref, vbuf_ref, rs_ref, hs_ref, bias_ref, dblk_ref, qdec_ref,
                         kdec_ref, lb_ref, gain_ref, padmask_ref, tril_ref, sel_ref, onesbd_ref, bdf_ref,
                         mix_ref, rso_ref, hso_ref, kall_ref, vall_ref, oint_ref, b_ref, q_ref, *, l):
    p = p_ref[...]
    half = lax.broadcasted_iota(I32, (GQ_A * T_PAD, 1), 0) < T_PAD
    kall_ref[...] = jnp.zeros_like(kall_ref)
    vall_ref[...] = jnp.zeros_like(vall_ref)

    for bb in range(GB):
        r0 = bb * T_PAD
        kall_ref[0:WINDOW] = kbuf_ref[bb]
        vall_ref[0:WINDOW] = vbuf_ref[bb]
        kall_ref[WINDOW:WINDOW + T_PAD] = p[r0:r0 + T_PAD, C_KA:C_KA + KV_A]
        vall_ref[WINDOW:WINDOW + T_PAD] = p[r0:r0 + T_PAD, C_VA:C_VA + KV_A]
        kall, vall = kall_ref[...], vall_ref[...]
        for kh in range(KVH_A):
            q = jnp.concatenate([p[r0:r0 + T_PAD, C_QA + (kh * GQ_A + g) * HD_A:C_QA + (kh * GQ_A + g + 1) * HD_A]
                                 for g in range(GQ_A)], axis=0)
            s = _mm_nt(q, kall[:, kh * HD_A:(kh + 1) * HD_A]) * (HD_A ** -0.5) + bias_ref[kh]
            sinkcol = jnp.where(half, scal_ref[l, kh * GQ_A], scal_ref[l, kh * GQ_A + 1])
            o = _softmax_sink_pv(s, sinkcol, vall[:, kh * HD_A:(kh + 1) * HD_A])
            for g in range(GQ_A):
                c0 = (kh * GQ_A + g) * HD_A
                mix_ref[r0:r0 + T_PAD, c0:c0 + HD_A] = o[g * T_PAD:(g + 1) * T_PAD]

    kd_t = (p[:, C_KB:C_KB + H_B * DK_B] * kdec_ref[...]).T
    qd = p[:, C_QB:C_QB + H_B * DK_B] * qdec_ref[...]
    for h in range(H_B):
        q = p[:, C_QB + h * DK_B:C_QB + (h + 1) * DK_B]
        k = p[:, C_KB + h * DK_B:C_KB + (h + 1) * DK_B] * (DK_B ** -0.5)
        v = p[:, C_VB + h * DV_B:C_VB + (h + 1) * DV_B]
        a = _mm_nt(q, k) * dblk_ref[h]
        o_intra = _mm(a, v)
        for bb in range(GB):
            r0 = bb * T_PAD
            s0 = rs_ref[bb, h]
            oint_ref[r0:r0 + T_PAD, 0:DV_B] = _mm(qd[r0:r0 + T_PAD, h * DK_B:(h + 1) * DK_B], s0)
            rso_ref[bb, h] = (s0 * scal_ref[l, H_A + h]
                              + _mm(kd_t[h * DK_B:(h + 1) * DK_B] * _col_mask(BLK, T_PAD, bb), v))
        o = o_intra + oint_ref[:, 0:DV_B]
        c0 = H_A * HD_A + h * DV_B
        mix_ref[:, c0:c0 + DV_B] = _group_norm_gate(o, p[:, C_GB + h * DV_B:C_GB + (h + 1) * DV_B])

    o_diag, qe, k2, ebl = _hgrn_front(p, lb_ref[0], padmask_ref[...], tril_ref[...], sel_ref[...], onesbd_ref[...],
                                      b_ref, q_ref, T_PAD)
    k2_t, ebl_t = k2.T, ebl.T
    v = p[:, C_IC:C_IC + HC]
    for bb in range(GB):
        r0 = bb * T_PAD
        k2m = k2_t * _col_mask(BLK, T_PAD, bb)
        for h in range(H_C):
            s0 = hs_ref[bb, h]
            oint_ref[r0:r0 + T_PAD, h * DV_C:(h + 1) * DV_C] = _mm(qe[r0:r0 + T_PAD, h * DK_C:(h + 1) * DK_C], s0)
            hso_ref[bb, h] = (ebl_t[h * DK_C:(h + 1) * DK_C, r0:r0 + 1] * s0
                              + _mm(k2m[h * DK_C:(h + 1) * DK_C], v[:, h * DV_C:(h + 1) * DV_C]))
    o = oint_ref[...] + o_diag
    c0 = H_A * HD_A + H_B * DV_B
    mix_ref[:, c0:c0 + HC] = _hgrn_finish(o, p, gain_ref[0], bdf_ref[...])


def _mixer_sample(proj8, scal, kbuf, vbuf, rstate, hstate, lb, gain, l, bs, tabs, htab):
    nb = bs // GB
    full = lambda a: pl.BlockSpec(a.shape, (lambda i: (0, 0)) if a.ndim == 2 else (lambda i: (0, 0, 0)))
    layer3 = lambda i: (l, 0, 0)
    return pl.pallas_call(
        functools.partial(_mixer_sample_kernel, l=l),
        grid=(nb,),
        in_specs=[pl.BlockSpec(memory_space=pltpu.SMEM),
                  pl.BlockSpec((BLK, D_PROJ), lambda i: (i, 0)),
                  pl.BlockSpec((GB, WINDOW, KV_A), lambda i: (l * nb + i, 0, 0)),
                  pl.BlockSpec((GB, WINDOW, KV_A), lambda i: (l * nb + i, 0, 0)),
                  pl.BlockSpec((GB, H_B, DK_B, DV_B), lambda i: (l * nb + i, 0, 0, 0)),
                  pl.BlockSpec((GB, H_C, DK_C, DV_C), lambda i: (l * nb + i, 0, 0, 0)),
                  full(tabs["bias"]), full(tabs["dblk"]), full(tabs["qdec"]), full(tabs["kdec"]),
                  pl.BlockSpec((1, 1, HC), layer3), pl.BlockSpec((1, 1, HC), layer3), full(tabs["padmask"]),
                  full(htab["tril"]), full(htab["sel"]), full(htab["onesbd"]), full(htab["bdf"])],
        out_specs=[pl.BlockSpec((BLK, D_MODEL), lambda i: (i, 0)),
                   pl.BlockSpec((GB, H_B, DK_B, DV_B), lambda i: (i, 0, 0, 0)),
                   pl.BlockSpec((GB, H_C, DK_C, DV_C), lambda i: (i, 0, 0, 0))],
        out_shape=[jax.ShapeDtypeStruct((bs * T_PAD, D_MODEL), F32),
                   jax.ShapeDtypeStruct((bs, H_B, DK_B, DV_B), F32),
                   jax.ShapeDtypeStruct((bs, H_C, DK_C, DV_C), F32)],
        scratch_shapes=[pltpu.VMEM((2 * WINDOW, KV_A), F32), pltpu.VMEM((2 * WINDOW, KV_A), F32),
                        pltpu.VMEM((BLK, HC), F32), pltpu.VMEM((BLK, HC), F32), pltpu.VMEM((BLK, HC), F32)],
        compiler_params=_cparams("parallel"),
        name="mixer_sample",
    )(scal, proj8, kbuf, vbuf, rstate, hstate, tabs["bias"], tabs["dblk"], tabs["qdec"], tabs["kdec"],
      lb, gain, tabs["padmask"], htab["tril"], htab["sel"], htab["onesbd"], htab["bdf"])


def _attend_heads(q, mk_heads, mv_heads):
    outs = []
    for h in range(XH):
        s = _mm_nt(q[:, h * XHD:(h + 1) * XHD], mk_heads[h]) * (XHD ** -0.5)
        e = jnp.exp(s - jnp.max(s, axis=-1, keepdims=True))
        outs.append(_mm(e, mv_heads[h]) / jnp.sum(e, axis=-1, keepdims=True))
    return jnp.concatenate(outs, axis=-1)


def _attend_memory(q, mk, mv):
    return _attend_heads(q, [mk[:, h * XHD:(h + 1) * XHD] for h in range(XH)],
                         [mv[:, h * XHD:(h + 1) * XHD] for h in range(XH)])


def _xattn_prompt_kernel(x_ref, g_ref, wq_ref, wo_ref, mk_ref, mv_ref, o_ref):
    x = x_ref[...]
    q = jnp.dot(_rms(x, g_ref[0]).astype(BF16), wq_ref[0], preferred_element_type=F32)
    o = _attend_memory(q, mk_ref[...], mv_ref[...])
    o_ref[...] = x + jnp.dot(o.astype(BF16), wo_ref[0], preferred_element_type=F32)


def _xattn_prompt(x, g, wq, wo, mk, mv, l, bp, t_len):
    tq = _pick(t_len, (512, 256, 128))
    nt = t_len // tq
    dq = XH * XHD
    layer3 = lambda b, t: (l, 0, 0)
    return pl.pallas_call(
        _xattn_prompt_kernel,
        grid=(bp, nt),
        in_specs=[pl.BlockSpec((tq, D_MODEL), lambda b, t: (b * nt + t, 0)),
                  pl.BlockSpec((1, 1, D_MODEL), layer3),
                  pl.BlockSpec((1, D_MODEL, dq), layer3),
                  pl.BlockSpec((1, dq, D_MODEL), layer3),
                  pl.BlockSpec((N_MEM, dq), lambda b, t: (b, 0)),
                  pl.BlockSpec((N_MEM, dq), lambda b, t: (b, 0))],
        out_specs=pl.BlockSpec((tq, D_MODEL), lambda b, t: (b * nt + t, 0)),
        out_shape=jax.ShapeDtypeStruct(x.shape, F32),
        input_output_aliases={0: 0},
        compiler_params=_cparams("parallel", "parallel"),
        name="xattn_prompt",
    )(x, g, wq, wo, mk, mv)


XB = 4


def _xattn_sample_kernel(q_ref, mk_ref, mv_ref, o_ref):
    for bb in range(XB):
        r0 = bb * T_PAD
        o_ref[r0:r0 + T_PAD] = _attend_heads(q_ref[r0:r0 + T_PAD],
                                             [mk_ref[bb, :, h, :] for h in range(XH)],
                                             [mv_ref[bb, :, h, :] for h in range(XH)])


def _xattn_sample(q8, mk, mv, l, bs):
    dq = XH * XHD
    nb = bs // XB
    return pl.pallas_call(
        _xattn_sample_kernel,
        grid=(nb,),
        in_specs=[pl.BlockSpec((XB * T_PAD, dq), lambda i: (i, 0)),
                  pl.BlockSpec((XB, N_MEM, XH, XHD), lambda i: (l * nb + i, 0, 0, 0)),
                  pl.BlockSpec((XB, N_MEM, XH, XHD), lambda i: (l * nb + i, 0, 0, 0))],
        out_specs=pl.BlockSpec((XB * T_PAD, dq), lambda i: (i, 0)),
        out_shape=jax.ShapeDtypeStruct((bs * T_PAD, dq), F32),
        compiler_params=_cparams("parallel"),
        name="xattn_sample",
    )(q8, mk, mv)


def _ffn_kernel(be_ref, nused_ref, x_ref, g_ref, w1_ref, w3_ref, w2_ref, o_ref, xn_ref, *, dense):
    i, j = pl.program_id(0), pl.program_id(1)

    @pl.when(j == 0)
    def _():
        x = x_ref[...]
        if dense:
            xn_ref[...] = _rms(x, g_ref[0]).astype(BF16)
            o_ref[...] = x
        else:
            xn_ref[...] = x.astype(BF16)
            o_ref[...] = jnp.zeros_like(o_ref)

    @pl.when(i < nused_ref[0])
    def _():
        xn = xn_ref[...]
        h1 = jnp.dot(xn, w1_ref[0], preferred_element_type=F32)
        h3 = jnp.dot(xn, w3_ref[0], preferred_element_type=F32)
        o_ref[...] += jnp.dot((_silu(h1) * h3).astype(BF16), w2_ref[0], preferred_element_type=F32)


def _ffn(x, g, l, w1, w3, w2, block_e, nused, tm, dense):
    rows = x.shape[0]
    dff = w1.shape[2]
    tf = _pick(dff, (512, 256, 128))
    nf = dff // tf

    def xmap(i, j, be, nu):
        return (jnp.minimum(i, nu[0] - 1), 0)

    def w13map(i, j, be, nu):
        return (be[jnp.minimum(i, nu[0] - 1)], 0, jnp.where(i < nu[0], j, nf - 1))

    def w2map(i, j, be, nu):
        return (be[jnp.minimum(i, nu[0] - 1)], jnp.where(i < nu[0], j, nf - 1), 0)

    grid_spec = pltpu.PrefetchScalarGridSpec(
        num_scalar_prefetch=2,
        grid=(rows // tm, nf),
        in_specs=[pl.BlockSpec((tm, D_MODEL), xmap),
                  pl.BlockSpec((1, 1, D_MODEL), lambda i, j, be, nu: (l, 0, 0)),
                  pl.BlockSpec((1, D_MODEL, tf), w13map),
                  pl.BlockSpec((1, D_MODEL, tf), w13map),
                  pl.BlockSpec((1, tf, D_MODEL), w2map)],
        out_specs=pl.BlockSpec((tm, D_MODEL), lambda i, j, be, nu: (i, 0)),
        scratch_shapes=[pltpu.VMEM((tm, D_MODEL), BF16)],
    )
    return pl.pallas_call(
        functools.partial(_ffn_kernel, dense=dense),
        grid_spec=grid_spec,
        out_shape=jax.ShapeDtypeStruct((rows, D_MODEL), F32),
        input_output_aliases={2: 0} if dense else {},
        compiler_params=_cparams("arbitrary", "arbitrary"),
        name="ffn_dense" if dense else "ffn_experts",
    )(block_e, nused, x, g, w1, w3, w2)


def _router_kernel(x_ref, g_ref, wr_ref, h_ref, idx_ref, gate_ref):
    h = _rms(x_ref[...], g_ref[0])
    h_ref[...] = h
    logits = _mm_hi(h, wr_ref[...])
    lane = lax.broadcasted_iota(I32, logits.shape, 1)
    logits = jnp.where(lane < N_EXPERTS, logits, NEG_INF)
    m1 = jnp.max(logits, axis=-1, keepdims=True)
    i1 = jnp.min(jnp.where(logits == m1, lane, LANES), axis=-1, keepdims=True)
    rest = jnp.where(lane == i1, NEG_INF, logits)
    m2 = jnp.max(rest, axis=-1, keepdims=True)
    i2 = jnp.min(jnp.where(rest == m2, lane, LANES), axis=-1, keepdims=True)
    e2 = jnp.exp(m2 - m1)
    den = 1.0 + e2
    idx_ref[...] = jnp.where(lane == 0, i1, jnp.where(lane == 1, i2, 0))
    gate_ref[...] = jnp.where(lane == 0, 1.0 / den, jnp.where(lane == 1, e2 / den, 0.0))


def _router(x, g, l, w_router):
    n = x.shape[0]
    tm = _pick(n, ROW_TILES)
    wr = jnp.pad(w_router, ((0, 0), (0, LANES - N_EXPERTS)))
    return pl.pallas_call(
        _router_kernel,
        grid=(n // tm,),
        in_specs=[pl.BlockSpec((tm, D_MODEL), lambda i: (i, 0)),
                  pl.BlockSpec((1, 1, D_MODEL), lambda i: (l, 0, 0)),
                  pl.BlockSpec((D_MODEL, LANES), lambda i: (0, 0))],
        out_specs=[pl.BlockSpec((tm, D_MODEL), lambda i: (i, 0)),
                   pl.BlockSpec((tm, LANES), lambda i: (i, 0)),
                   pl.BlockSpec((tm, LANES), lambda i: (i, 0))],
        out_shape=[jax.ShapeDtypeStruct((n, D_MODEL), F32),
                   jax.ShapeDtypeStruct((n, LANES), I32),
                   jax.ShapeDtypeStruct((n, LANES), F32)],
        compiler_params=_cparams("parallel"),
        name="router",
    )(x, g, wr)


GATHER_UNROLL = 8


def _gather_rows_kernel(tok_ref, src_ref, o_ref, sem, *, tg):
    base = pl.program_id(0) * tg

    def issue(r, c):
        pltpu.make_async_copy(src_ref.at[pl.ds(tok_ref[base + r], 1)], o_ref.at[pl.ds(r, 1)], sem).start()
        return c

    lax.fori_loop(0, tg, issue, 0, unroll=GATHER_UNROLL)
    pltpu.make_async_copy(src_ref.at[pl.ds(0, tg)], o_ref, sem).wait()


def _gather_rows(src, tok, tg):
    rows = tok.shape[0]
    grid_spec = pltpu.PrefetchScalarGridSpec(
        num_scalar_prefetch=1,
        grid=(rows // tg,),
        in_specs=[pl.BlockSpec(memory_space=pl.ANY)],
        out_specs=pl.BlockSpec((tg, D_MODEL), lambda i, tok: (i, 0)),
        scratch_shapes=[pltpu.SemaphoreType.DMA],
    )
    return pl.pallas_call(
        functools.partial(_gather_rows_kernel, tg=tg),
        grid_spec=grid_spec,
        out_shape=jax.ShapeDtypeStruct((rows, D_MODEL), F32),
        compiler_params=_cparams("arbitrary"),
        name="moe_gather",
    )(tok, src)


def _expert_ffn_kernel(be_ref, nused_ref, tok_ref, h_ref, w1_ref, w3_ref, w2_ref, o_ref, xbuf_ref, xn_ref, sem,
                       *, tm):
    i, j = pl.program_id(0), pl.program_id(1)
    nu = nused_ref[0]

    def start_gather(blk, slot):
        base = blk * tm

        def issue(r, c):
            pltpu.make_async_copy(h_ref.at[pl.ds(tok_ref[base + r], 1)], xbuf_ref.at[slot, pl.ds(r, 1)],
                                  sem.at[slot]).start()
            return c

        lax.fori_loop(0, tm, issue, 0, unroll=GATHER_UNROLL)

    @pl.when(j == 0)
    def _():
        slot = i % 2

        @pl.when((i == 0) & (nu > 0))
        def _():
            start_gather(0, 0)

        @pl.when(i + 1 < nu)
        def _():
            start_gather(i + 1, 1 - slot)

        @pl.when(i < nu)
        def _():
            pltpu.make_async_copy(h_ref.at[pl.ds(0, tm)], xbuf_ref.at[slot], sem.at[slot]).wait()
            xn_ref[...] = xbuf_ref[slot].astype(BF16)

        o_ref[...] = jnp.zeros_like(o_ref)

    @pl.when(i < nu)
    def _():
        xn = xn_ref[...]
        h1 = jnp.dot(xn, w1_ref[0].astype(BF16), preferred_element_type=F32)
        h3 = jnp.dot(xn, w3_ref[0].astype(BF16), preferred_element_type=F32)
        o_ref[...] += jnp.dot((_silu(h1) * h3).astype(BF16), w2_ref[0].astype(BF16), preferred_element_type=F32)


def _expert_ffn(h, tok, w1, w3, w2, block_e, nused, tm):
    rows = tok.shape[0]
    dff = w1.shape[2]
    tf = _pick(dff, (512, 256, 128))
    nf = dff // tf

    def w13map(i, j, be, nu, tok):
        return (be[jnp.minimum(i, nu[0] - 1)], 0, jnp.where(i < nu[0], j, nf - 1))

    def w2map(i, j, be, nu, tok):
        return (be[jnp.minimum(i, nu[0] - 1)], jnp.where(i < nu[0], j, nf - 1), 0)

    grid_spec = pltpu.PrefetchScalarGridSpec(
        num_scalar_prefetch=3,
        grid=(rows // tm, nf),
        in_specs=[pl.BlockSpec(memory_space=pl.ANY),
                  pl.BlockSpec((1, D_MODEL, tf), w13map),
                  pl.BlockSpec((1, D_MODEL, tf), w13map),
                  pl.BlockSpec((1, tf, D_MODEL), w2map)],
        out_specs=pl.BlockSpec((tm, D_MODEL), lambda i, j, be, nu, tok: (i, 0)),
        scratch_shapes=[pltpu.VMEM((2, tm, D_MODEL), F32), pltpu.VMEM((tm, D_MODEL), BF16),
                        pltpu.SemaphoreType.DMA((2,))],
    )
    return pl.pallas_call(
        functools.partial(_expert_ffn_kernel, tm=tm),
        grid_spec=grid_spec,
        out_shape=jax.ShapeDtypeStruct((rows, D_MODEL), F32),
        compiler_params=_cparams("arbitrary", "arbitrary"),
        name="ffn_experts",
    )(block_e, nused, tok, h, w1, w3, w2)


def _combine_kernel(pos_ref, x_ref, gate_ref, y_ref, o_ref, buf_ref, sem, *, tc):
    base = pl.program_id(0) * tc

    def issue(r, c):
        for k in range(TOP_K):
            pltpu.make_async_copy(y_ref.at[pl.ds(pos_ref[(base + r) * TOP_K + k], 1)],
                                  buf_ref.at[k, pl.ds(r, 1)], sem).start()
        return c

    lax.fori_loop(0, tc, issue, 0, unroll=GATHER_UNROLL)
    for k in range(TOP_K):
        pltpu.make_async_copy(y_ref.at[pl.ds(0, tc)], buf_ref.at[k], sem).wait()
    gate = gate_ref[...]
    o_ref[...] = x_ref[...] + (buf_ref[0] * gate[:, 0:1] + buf_ref[1] * gate[:, 1:2])


def _combine(x, gate, y, pos, tc):
    n = x.shape[0]
    grid_spec = pltpu.PrefetchScalarGridSpec(
        num_scalar_prefetch=1,
        grid=(n // tc,),
        in_specs=[pl.BlockSpec((tc, D_MODEL), lambda i, pos: (i, 0)),
                  pl.BlockSpec((tc, LANES), lambda i, pos: (i, 0)),
                  pl.BlockSpec(memory_space=pl.ANY)],
        out_specs=pl.BlockSpec((tc, D_MODEL), lambda i, pos: (i, 0)),
        scratch_shapes=[pltpu.VMEM((TOP_K, tc, D_MODEL), F32), pltpu.SemaphoreType.DMA],
    )
    return pl.pallas_call(
        functools.partial(_combine_kernel, tc=tc),
        grid_spec=grid_spec,
        out_shape=jax.ShapeDtypeStruct(x.shape, F32),
        input_output_aliases={1: 0},
        compiler_params=_cparams("arbitrary"),
        name="moe_combine",
    )(pos, x, gate, y)


def _moe(x, g, l, w_router, w1, w3, w2, e0):
    n = x.shape[0]
    a = n * TOP_K
    tm = _pick(a, (1024,) + ROW_TILES)
    h, idx, gate = _router(x, g, l, w_router)
    e_flat = idx[:, :TOP_K].reshape(a)
    onehot = (e_flat[:, None] == jnp.arange(N_EXPERTS, dtype=I32)[None, :]).astype(I32)
    csum = jnp.cumsum(onehot, axis=0)
    counts = csum[-1]
    rank = jnp.sum((csum - onehot) * onehot, axis=1)
    padded = (counts + tm - 1) // tm * tm
    pend = jnp.cumsum(padded)
    pstart = pend - padded
    dest = (pstart[e_flat] + rank).astype(I32)
    nb = a // tm + N_EXPERTS
    tok = jnp.zeros((nb * tm,), I32).at[dest].set(jnp.repeat(jnp.arange(n, dtype=I32), TOP_K))
    block_start = jnp.arange(nb, dtype=I32) * tm
    block_e = jnp.minimum(jnp.sum((pend[None, :] <= block_start[:, None]).astype(I32), axis=1), N_EXPERTS - 1)
    nused = (pend[-1:] // tm).astype(I32)
    y = _expert_ffn(h, tok, w1, w3, w2, (block_e + e0).astype(I32), nused, tm)
    return _combine(x, gate, y, dest, _pick(n, (256, 128, 64, 32, 16, 8)))


def kernel(x_prompt, x_sample, cache_swa_k, cache_swa_v, state_ret, state_hgrn, cache_mem_k, cache_mem_v, mem_prompt, norm_mix, w_in, swa_sinks, hgrn_lb_logits, hgrn_norm, w_out, norm_xattn, norm_mem, wx_q, wx_k, wx_v, wx_o, norm_ffn, ffn_w1, ffn_w3, ffn_w2, moe_router, moe_w1, moe_w3, moe_w2, final_norm):
    bp, t_len, _ = x_prompt.shape
    bs, ts, _ = x_sample.shape
    depth = w_in.shape[0]
    n_p, n_s = bp * t_len, bs * ts
    n = n_p + n_s
    assert t_len % BLK == 0 and bs % GB == 0 and bs % XB == 0 and ts <= T_PAD
    assert cache_swa_k.shape[2] == WINDOW and ts < WINDOW
    dq = XH * XHD

    x = jnp.concatenate([x_prompt.reshape(n_p, D_MODEL), x_sample.reshape(n_s, D_MODEL)], axis=0)
    ptab, stab = _prompt_tables(), _sample_tables(ts)
    htab_p, htab_s = _hgrn_tables(SUB_P), _hgrn_tables(T_PAD)
    lb_p = jax.nn.softmax(hgrn_lb_logits.astype(F32), axis=0)
    lower_bounds = (jnp.cumsum(lb_p, axis=0) - lb_p[0]).reshape(depth, 1, HC)
    gains = hgrn_norm.astype(F32).reshape(depth, 1, HC)
    sinks = swa_sinks.astype(F32)
    scal_p = jnp.concatenate([sinks, jnp.broadcast_to(jnp.asarray(ptab["cdec"], F32), (depth, H_B))], axis=1)
    scal_s = jnp.concatenate([sinks, jnp.broadcast_to(jnp.asarray(stab["cdec"], F32), (depth, H_B))], axis=1)
    mem = mem_prompt.reshape(bp * N_MEM, D_MODEL)

    bf = lambda w: w.astype(BF16)
    w_in_b, w_out_b = bf(w_in), bf(w_out)
    wq_b, wk_b, wv_b, wo_b = bf(wx_q), bf(wx_k), bf(wx_v), bf(wx_o)
    f1_b, f3_b, f2_b = bf(ffn_w1), bf(ffn_w3), bf(ffn_w2)
    m1_b = moe_w1.astype(F32).reshape((-1,) + moe_w1.shape[2:])
    m3_b = moe_w3.astype(F32).reshape((-1,) + moe_w3.shape[2:])
    m2_b = moe_w2.astype(F32).reshape((-1,) + moe_w2.shape[2:])
    vec3 = lambda v: v.astype(F32).reshape(depth, 1, D_MODEL)
    n_mix, n_xat, n_mem, n_ffn = vec3(norm_mix), vec3(norm_xattn), vec3(norm_mem), vec3(norm_ffn)

    kbuf = cache_swa_k.astype(F32).reshape(depth * bs, WINDOW, KV_A)
    vbuf = cache_swa_v.astype(F32).reshape(depth * bs, WINDOW, KV_A)
    rstate = state_ret.astype(F32).reshape((depth * bs,) + state_ret.shape[2:])
    hstate = state_hgrn.astype(F32).reshape((depth * bs,) + state_hgrn.shape[2:])
    cmk = cache_mem_k.astype(F32).reshape(depth * bs, N_MEM, XH, XHD)
    cmv = cache_mem_v.astype(F32).reshape(depth * bs, N_MEM, XH, XHD)

    def pad_tokens(a):
        return jnp.pad(a.reshape(bs, ts, -1), ((0, 0), (0, T_PAD - ts), (0, 0))).reshape(bs * T_PAD, -1)

    def unpad_tokens(a):
        return a.reshape(bs, T_PAD, -1)[:, :ts].reshape(bs * ts, -1)

    outs = {k: [] for k in ("pk", "pv", "pr", "pc", "pmk", "pmv", "sk", "sv", "sr", "sc")}
    for l in range(depth):
        proj = _rms_matmul(x, n_mix, w_in_b, l)
        mix_p, kkeep, vkeep, ret_p, hg_p = _mixer_prompt(proj, scal_p, lower_bounds, gains, l, bp, t_len, ptab, htab_p)
        proj_s = lax.slice_in_dim(proj, n_p, n, axis=0)
        mix_s, ret_s, hg_s = _mixer_sample(pad_tokens(proj_s), scal_s, kbuf, vbuf, rstate, hstate,
                                           lower_bounds, gains, l, bs, stab, htab_s)
        x = _matmul_res(mix_p, w_out_b, l, x, 0)
        x = _matmul_res(unpad_tokens(mix_s), w_out_b, l, x, n_p)

        outs["pk"].append(kkeep.reshape(bp, WINDOW, KVH_A, HD_A))
        outs["pv"].append(vkeep.reshape(bp, WINDOW, KVH_A, HD_A))
        outs["pr"].append(ret_p)
        outs["pc"].append(jnp.stack([hg_p[:, h * DK_C:(h + 1) * DK_C, h * DV_C:(h + 1) * DV_C]
                                     for h in range(H_C)], axis=1))
        new_k = proj_s[:, C_KA:C_KA + KV_A].reshape(bs, ts, KVH_A, HD_A)
        new_v = proj_s[:, C_VA:C_VA + KV_A].reshape(bs, ts, KVH_A, HD_A)
        outs["sk"].append(jnp.concatenate([cache_swa_k[l][:, ts:].astype(F32), new_k], axis=1))
        outs["sv"].append(jnp.concatenate([cache_swa_v[l][:, ts:].astype(F32), new_v], axis=1))
        outs["sr"].append(ret_s)
        outs["sc"].append(hg_s)

        mk = _rms_matmul(mem, n_mem, wk_b, l)
        mv = _rms_matmul(mem, n_mem, wv_b, l)
        outs["pmk"].append(mk.reshape(bp, N_MEM, XH, XHD))
        outs["pmv"].append(mv.reshape(bp, N_MEM, XH, XHD))
        x = _xattn_prompt(x, n_xat, wq_b, wo_b, mk, mv, l, bp, t_len)
        q_s = _rms_matmul(x, n_xat, wq_b, l, row_off=n_p, rows=n_s)
        o_s = _xattn_sample(pad_tokens(q_s), cmk, cmv, l, bs)
        x = _matmul_res(unpad_tokens(o_s), wo_b, l, x, n_p)

        e = l // 2
        if l % 2 == 0:
            tm = _pick(n, (1536, 768) + ROW_TILES)
            nb = n // tm
            x = _ffn(x, n_ffn, l, f1_b, f3_b, f2_b, jnp.full((nb,), e, I32), jnp.full((1,), nb, I32), tm, dense=True)
        else:
            x = _moe(x, n_ffn, l, moe_router[e], m1_b, m3_b, m2_b, e * N_EXPERTS)

    y_p = _rmsnorm_rows(x, final_norm, 0, n_p).reshape(bp, t_len, D_MODEL)
    y_s = _rmsnorm_rows(x, final_norm, n_p, n_s).reshape(bs, ts, D_MODEL)
    st = lambda k: jnp.stack(outs[k])
    return (y_p, y_s, st("pk"), st("pv"), st("pr"), st("pc"), st("pmk"), st("pmv"),
            st("sk"), st("sv"), st("sr"), st("sc"))
```

```python
import functools

import numpy as np
import jax
import jax.numpy as jnp
from jax import lax
from jax.experimental import pallas as pl
from jax.experimental.pallas import tpu as pltpu

F32, BF16, I32 = jnp.float32, jnp.bfloat16, jnp.int32
HIGHEST = lax.Precision.HIGHEST
NEG_INF = float("-inf")

D_MODEL = 1024
WINDOW = 128
H_A, KVH_A, GQ_A, HD_A = 4, 2, 2, 64
ALIBI_MAX_EXP = 8.0
H_B, DK_B, DV_B = 4, 64, 128
RET_CHUNK = 128
H_C, DK_C, DV_C = 4, 64, 64
N_MEM, XH, XHD = 256, 4, 128
N_EXPERTS, TOP_K = 8, 2
EPS = 1e-6
D_PROJ = 3072
C_QA, C_KA, C_VA = 0, 256, 384
C_QB, C_KB, C_VB, C_GB = 512, 768, 1024, 1536
C_FC, C_QC, C_IC, C_GC = 2048, 2304, 2560, 2816
KV_A = KVH_A * HD_A
HC = H_C * DK_C

BLK = 128
SUB_P = 16
T_PAD = 8
GB = BLK // T_PAD
LANES = 128
VMEM_LIMIT = 56 * 1024 * 1024
ROW_TILES = (512, 256, 128, 64, 32, 16, 8)


def _pick(n, cands):
    for c in cands:
        if n % c == 0:
            return c
    raise ValueError(f"no tile for {n}")


def _row_tile(rows, row_off):
    return _pick(int(np.gcd(rows, row_off)) if row_off else rows, ROW_TILES)


def _cparams(*sem):
    return pltpu.CompilerParams(dimension_semantics=sem, vmem_limit_bytes=VMEM_LIMIT)


def _mm(a, b):
    return jnp.dot(a.astype(BF16), b.astype(BF16), preferred_element_type=F32)


def _mm_nt(a, b):
    return lax.dot_general(a.astype(BF16), b.astype(BF16), (((1,), (1,)), ((), ())),
                           preferred_element_type=F32)


def _mm_hi(a, b):
    return jnp.dot(a, b, precision=HIGHEST, preferred_element_type=F32)


def _sigmoid(x):
    return 1.0 / (1.0 + jnp.exp(-x))


def _silu(x):
    return x * _sigmoid(x)


def _rms(x, g):
    return x * lax.rsqrt(jnp.mean(x * x, axis=-1, keepdims=True) + EPS) * g


def _rms_matmul_kernel(x_ref, g_ref, w_ref, o_ref, xn_ref):
    @pl.when(pl.program_id(1) == 0)
    def _():
        xn_ref[...] = _rms(x_ref[...], g_ref[0]).astype(BF16)

    o_ref[...] = jnp.dot(xn_ref[...], w_ref[0], preferred_element_type=F32)


def _rms_matmul(x, g, w, l, row_off=0, rows=None):
    n, k = x.shape
    m = w.shape[2]
    rows = n if rows is None else rows
    tm = _row_tile(rows, row_off)
    tn = _pick(m, (3072, 1536, 1024, 512, 256, 128))
    off = row_off // tm
    return pl.pallas_call(
        _rms_matmul_kernel,
        grid=(rows // tm, m // tn),
        in_specs=[pl.BlockSpec((tm, k), lambda i, j: (i + off, 0)),
                  pl.BlockSpec((1, 1, k), lambda i, j: (l, 0, 0)),
                  pl.BlockSpec((1, k, tn), lambda i, j: (l, 0, j))],
        out_specs=pl.BlockSpec((tm, tn), lambda i, j: (i, j)),
        out_shape=jax.ShapeDtypeStruct((rows, m), F32),
        scratch_shapes=[pltpu.VMEM((tm, k), BF16)],
        compiler_params=_cparams("parallel", "arbitrary"),
        name="rms_matmul",
    )(x, g, w)


def _matmul_res_kernel(a_ref, w_ref, x_ref, o_ref):
    o_ref[...] = x_ref[...] + jnp.dot(a_ref[...].astype(BF16), w_ref[0], preferred_element_type=F32)


def _matmul_res(a, w, l, x, row_off):
    rows, k = a.shape
    m = w.shape[2]
    tm = _row_tile(rows, row_off)
    off = row_off // tm
    return pl.pallas_call(
        _matmul_res_kernel,
        grid=(rows // tm,),
        in_specs=[pl.BlockSpec((tm, k), lambda i: (i, 0)),
                  pl.BlockSpec((1, k, m), lambda i: (l, 0, 0)),
                  pl.BlockSpec((tm, m), lambda i: (i + off, 0))],
        out_specs=pl.BlockSpec((tm, m), lambda i: (i + off, 0)),
        out_shape=jax.ShapeDtypeStruct(x.shape, F32),
        input_output_aliases={2: 0},
        compiler_params=_cparams("parallel"),
        name="matmul_res",
    )(a, w, x)


def _rmsnorm_kernel(x_ref, g_ref, o_ref):
    o_ref[...] = _rms(x_ref[...], g_ref[...])


def _rmsnorm_rows(x, g, row_off, rows):
    n, k = x.shape
    tm = _row_tile(rows, row_off)
    off = row_off // tm
    return pl.pallas_call(
        _rmsnorm_kernel,
        grid=(rows // tm,),
        in_specs=[pl.BlockSpec((tm, k), lambda i: (i + off, 0)),
                  pl.BlockSpec((1, k), lambda i: (0, 0))],
        out_specs=pl.BlockSpec((tm, k), lambda i: (i, 0)),
        out_shape=jax.ShapeDtypeStruct((rows, k), F32),
        compiler_params=_cparams("parallel"),
        name="final_rmsnorm",
    )(x, g.reshape(1, k))


def _retention_gamma():
    return 1.0 - np.exp2(-5.0 - np.arange(H_B, dtype=np.float64))


def _alibi_slopes():
    return np.exp2(-(ALIBI_MAX_EXP / H_A) * np.arange(1, H_A + 1, dtype=np.float64))


def _head_cols(vals_hr, width):
    return np.repeat(vals_hr.T[:, :, None], width, axis=2).reshape(vals_hr.shape[1], -1)


def _hgrn_tables(sub):
    r = np.arange(BLK)
    same = (r[:, None] // sub) == (r[None, :] // sub)
    tril = (same & (r[None, :] <= r[:, None])).astype(np.float32)
    pr = np.arange(BLK * sub)
    sel = (r[:, None] == (pr // sub)[None, :]).astype(np.float32)
    hh = np.arange(HC) // DK_C
    bd = (hh[:, None] == hh[None, :]).astype(np.float32)
    return dict(tril=jnp.asarray(tril), sel=jnp.asarray(sel, dtype=BF16),
                onesbd=jnp.asarray(bd, dtype=BF16), bdf=jnp.asarray(bd))


def _prompt_tables():
    gam = _retention_gamma()
    i = np.arange(RET_CHUNK, dtype=np.float64)
    diff = i[:, None] - i[None, :]
    dmat = np.where(diff >= 0, gam[:, None, None] ** np.maximum(diff, 0.0), 0.0)
    qdec = _head_cols(gam[:, None] ** (i + 1.0), DK_B)
    kdec = _head_cols(gam[:, None] ** (RET_CHUNK - 1.0 - i), DK_B) * (DK_B ** -0.5)
    cdec = gam ** RET_CHUNK
    slopes = _alibi_slopes().reshape(KVH_A, GQ_A)
    qi = np.tile(np.arange(WINDOW), GQ_A)[:, None]
    qg = np.repeat(np.arange(GQ_A), WINDOW)[:, None]
    j = np.arange(2 * WINDOW)[None, :]
    rel = WINDOW + qi - j
    ok = (rel >= 0) & (rel < WINDOW)
    bias = np.stack([np.where(ok, -slopes[kh][qg] * rel, NEG_INF) for kh in range(KVH_A)])
    bias_first = np.where(j < WINDOW, NEG_INF, bias)
    return dict(dmat=jnp.asarray(dmat, dtype=F32), qdec=jnp.asarray(qdec, dtype=F32),
                kdec=jnp.asarray(kdec, dtype=F32), cdec=cdec,
                bias=jnp.asarray(np.stack([bias_first, bias]), dtype=F32))


def _sample_tables(t):
    gam = _retention_gamma()
    r = np.arange(BLK)
    ti, bi = r % T_PAD, r // T_PAD
    diff = (ti[:, None] - ti[None, :]).astype(np.float64)
    same = bi[:, None] == bi[None, :]
    dblk = np.where(same & (diff >= 0), gam[:, None, None] ** np.maximum(diff, 0.0), 0.0)
    qdec = _head_cols(gam[:, None] ** (ti + 1.0), DK_B)
    kdec = _head_cols(gam[:, None] ** (t - 1.0 - ti), DK_B) * (DK_B ** -0.5)
    cdec = gam ** t
    slopes = _alibi_slopes().reshape(KVH_A, GQ_A)
    qt = np.tile(np.arange(T_PAD), GQ_A)[:, None]
    qg = np.repeat(np.arange(GQ_A), T_PAD)[:, None]
    c = np.arange(2 * WINDOW)[None, :]
    rel = qt + WINDOW - c
    ok = (rel >= 0) & (rel < WINDOW) & (c < WINDOW + t)
    bias = np.stack([np.where(ok, -slopes[kh][qg] * rel, NEG_INF) for kh in range(KVH_A)])
    padmask = np.repeat((ti < t).astype(np.float32)[:, None], HC, axis=1)
    return dict(dblk=jnp.asarray(dblk, dtype=F32), qdec=jnp.asarray(qdec, dtype=F32),
                kdec=jnp.asarray(kdec, dtype=F32), cdec=cdec, bias=jnp.asarray(bias, dtype=F32),
                padmask=jnp.asarray(padmask))


def _tile_j(x, sub):
    nsub = x.shape[0] // sub
    y = jnp.broadcast_to(x.reshape(nsub, 1, sub, x.shape[1]), (nsub, sub, sub, x.shape[1]))
    return y.reshape(nsub * sub * sub, x.shape[1])


def _row_bcast(ref, r, sub):
    return jnp.broadcast_to(ref[r:r + 1, :], (sub, ref.shape[1]))


def _hgrn_front(p, lb, padmask, tril, sel, onesbd, b_ref, q_ref, sub):
    fc, qc, ic = p[:, C_FC:C_FC + HC], p[:, C_QC:C_QC + HC], p[:, C_IC:C_IC + HC]
    f = lb + (1.0 - lb) * _sigmoid(fc)
    kk = 1.0 - f
    g = jnp.log(f)
    if padmask is not None:
        kk = kk * padmask
        g = g * padmask
    qq = _silu(qc)
    b = _mm_hi(tril, g)
    b_ref[...] = b
    q_ref[...] = qq
    jrow = lax.broadcasted_iota(I32, (sub, 1), 0)
    bi = jnp.concatenate([_row_bcast(b_ref, r, sub) for r in range(BLK)], axis=0)
    qi = jnp.concatenate([jnp.where(jrow <= r % sub, _row_bcast(q_ref, r, sub), 0.0) for r in range(BLK)], axis=0)
    w = jnp.exp(jnp.minimum(bi - _tile_j(b, sub), 0.0))
    wm = qi * _tile_j(kk, sub) * w
    ab = jnp.dot(wm.astype(BF16), onesbd, preferred_element_type=F32)
    cm = ab * _tile_j(ic, sub)
    o_diag = jnp.dot(sel, cm.astype(BF16), preferred_element_type=F32)
    blrep = jnp.concatenate([_row_bcast(b_ref, i * sub + sub - 1, sub) for i in range(BLK // sub)], axis=0)
    return o_diag, qq * jnp.exp(b), kk * jnp.exp(blrep - b), jnp.exp(blrep)


def _hgrn_finish(o, p, gain, bdf):
    ms = _mm_hi(o * o, bdf) * (1.0 / DV_C)
    return o * lax.rsqrt(ms + EPS) * gain * _silu(p[:, C_GC:C_GC + HC])


def _group_norm_gate(o, gate):
    c = o - jnp.mean(o, axis=-1, keepdims=True)
    return c * lax.rsqrt(jnp.mean(c * c, axis=-1, keepdims=True) + EPS) * _silu(gate)


def _softmax_sink_pv(s, sinkcol, v):
    m = jnp.maximum(jnp.max(s, axis=-1, keepdims=True), sinkcol)
    e = jnp.exp(s - m)
    den = jnp.sum(e, axis=-1, keepdims=True) + jnp.exp(sinkcol - m)
    return _mm(e, v) / den


def _col_mask(width, sub, idx):
    return (lax.broadcasted_iota(I32, (1, width), 1) // sub == idx).astype(F32)


def _mixer_prompt_kernel(scal_ref, p_ref, prev_ref, bias_ref, dmat_ref, qdec_ref, kdec_ref, lb_ref, gain_ref,
                         tril_ref, sel_ref, onesbd_ref, bdf_ref,
                         mix_ref, kkeep_ref, vkeep_ref, rets_ref, hgs_ref,
                         sret_ref, shg_ref, b_ref, q_ref, *, l):
    t = pl.program_id(1)

    @pl.when(t == 0)
    def _():
        sret_ref[...] = jnp.zeros_like(sret_ref)
        shg_ref[...] = jnp.zeros_like(shg_ref)

    p = p_ref[...]
    prev = prev_ref[...]
    half = lax.broadcasted_iota(I32, (GQ_A * BLK, 1), 0) < BLK

    for kh in range(KVH_A):
        q = jnp.concatenate([p[:, C_QA + (kh * GQ_A + g) * HD_A:C_QA + (kh * GQ_A + g + 1) * HD_A]
                             for g in range(GQ_A)], axis=0)
        kband = jnp.concatenate([prev[:, kh * HD_A:(kh + 1) * HD_A],
                                 p[:, C_KA + kh * HD_A:C_KA + (kh + 1) * HD_A]], axis=0)
        vband = jnp.concatenate([prev[:, KV_A + kh * HD_A:KV_A + (kh + 1) * HD_A],
                                 p[:, C_VA + kh * HD_A:C_VA + (kh + 1) * HD_A]], axis=0)
        s = _mm_nt(q, kband) * (HD_A ** -0.5) + bias_ref[0, kh]
        sinkcol = jnp.where(half, scal_ref[l, kh * GQ_A], scal_ref[l, kh * GQ_A + 1])
        o = _softmax_sink_pv(s, sinkcol, vband)
        for g in range(GQ_A):
            c0 = (kh * GQ_A + g) * HD_A
            mix_ref[:, c0:c0 + HD_A] = o[g * BLK:(g + 1) * BLK]

    kd_t = (p[:, C_KB:C_KB + H_B * DK_B] * kdec_ref[...]).T
    qd = p[:, C_QB:C_QB + H_B * DK_B] * qdec_ref[...]
    for h in range(H_B):
        q = p[:, C_QB + h * DK_B:C_QB + (h + 1) * DK_B]
        k = p[:, C_KB + h * DK_B:C_KB + (h + 1) * DK_B] * (DK_B ** -0.5)
        v = p[:, C_VB + h * DV_B:C_VB + (h + 1) * DV_B]
        a = _mm_nt(q, k) * dmat_ref[h]
        s0 = sret_ref[h]
        o = _mm(a, v) + _mm(qd[:, h * DK_B:(h + 1) * DK_B], s0)
        sret_ref[h] = s0 * scal_ref[l, H_A + h] + _mm(kd_t[h * DK_B:(h + 1) * DK_B], v)
        c0 = H_A * HD_A + h * DV_B
        mix_ref[:, c0:c0 + DV_B] = _group_norm_gate(o, p[:, C_GB + h * DV_B:C_GB + (h + 1) * DV_B])

    o_diag, qe, k2, ebl = _hgrn_front(p, lb_ref[0], None, tril_ref[...], sel_ref[...], onesbd_ref[...],
                                      b_ref, q_ref, SUB_P)
    k2_t, ebl_t = k2.T, ebl.T
    v = p[:, C_IC:C_IC + HC]
    bdf = bdf_ref[...]
    st = shg_ref[...]
    o_inter = []
    for i in range(BLK // SUB_P):
        o_inter.append(_mm(qe[i * SUB_P:(i + 1) * SUB_P], st))
        u = _mm(k2_t * _col_mask(BLK, SUB_P, i), v) * bdf
        st = ebl_t[:, i * SUB_P:i * SUB_P + 1] * st + u
    shg_ref[...] = st
    o = jnp.concatenate(o_inter, axis=0) + o_diag
    c0 = H_A * HD_A + H_B * DV_B
    mix_ref[:, c0:c0 + HC] = _hgrn_finish(o, p, gain_ref[0], bdf)

    @pl.when(t == pl.num_programs(1) - 1)
    def _():
        kkeep_ref[0] = p[:, C_KA:C_KA + KV_A]
        vkeep_ref[0] = p[:, C_VA:C_VA + KV_A]
        rets_ref[0] = sret_ref[...]
        hgs_ref[0] = shg_ref[...]


def _mixer_prompt(proj, scal, lb, gain, l, bp, t_len, tabs, htab):
    nb = t_len // BLK
    const2 = lambda b, t: (0, 0)
    const3 = lambda b, t: (0, 0, 0)
    layer3 = lambda b, t: (l, 0, 0)
    full = lambda a: pl.BlockSpec(a.shape, const2 if a.ndim == 2 else const3)
    return pl.pallas_call(
        functools.partial(_mixer_prompt_kernel, l=l),
        grid=(bp, nb),
        in_specs=[pl.BlockSpec(memory_space=pltpu.SMEM),
                  pl.BlockSpec((BLK, D_PROJ), lambda b, t: (b * nb + t, 0)),
                  pl.BlockSpec((BLK, 2 * KV_A), lambda b, t: (b * nb + jnp.maximum(t - 1, 0), 1)),
                  pl.BlockSpec((1, KVH_A, GQ_A * BLK, 2 * BLK), lambda b, t: (jnp.minimum(t, 1), 0, 0, 0)),
                  full(tabs["dmat"]), full(tabs["qdec"]), full(tabs["kdec"]),
                  pl.BlockSpec((1, 1, HC), layer3), pl.BlockSpec((1, 1, HC), layer3),
                  full(htab["tril"]), full(htab["sel"]), full(htab["onesbd"]), full(htab["bdf"])],
        out_specs=[pl.BlockSpec((BLK, D_MODEL), lambda b, t: (b * nb + t, 0)),
                   pl.BlockSpec((1, WINDOW, KV_A), lambda b, t: (b, 0, 0)),
                   pl.BlockSpec((1, WINDOW, KV_A), lambda b, t: (b, 0, 0)),
                   pl.BlockSpec((1, H_B, DK_B, DV_B), lambda b, t: (b, 0, 0, 0)),
                   pl.BlockSpec((1, HC, HC), lambda b, t: (b, 0, 0))],
        out_shape=[jax.ShapeDtypeStruct((bp * t_len, D_MODEL), F32),
                   jax.ShapeDtypeStruct((bp, WINDOW, KV_A), F32),
                   jax.ShapeDtypeStruct((bp, WINDOW, KV_A), F32),
                   jax.ShapeDtypeStruct((bp, H_B, DK_B, DV_B), F32),
                   jax.ShapeDtypeStruct((bp, HC, HC), F32)],
        scratch_shapes=[pltpu.VMEM((H_B, DK_B, DV_B), F32), pltpu.VMEM((HC, HC), F32),
                        pltpu.VMEM((BLK, HC), F32), pltpu.VMEM((BLK, HC), F32)],
        compiler_params=_cparams("parallel", "arbitrary"),
        name="mixer_prompt",
    )(scal, proj, proj, tabs["bias"], tabs["dmat"], tabs["qdec"], tabs["kdec"], lb, gain,
      htab["tril"], htab["sel"], htab["onesbd"], htab["bdf"])


def _mixer_sample_kernel(scal_ref, p_ref, kbuf_ref, vbuf_ref, rs_ref, hs_ref, bias_ref, dblk_ref, qdec_ref,
                         kdec_ref, lb_ref, gain_ref, padmask_ref, tril_ref, sel_ref, onesbd_ref, bdf_ref,
                         mix_ref, rso_ref, hso_ref, kall_ref, vall_ref, oint_ref, b_ref, q_ref, *, l):
    p = p_ref[...]
    half = lax.broadcasted_iota(I32, (GQ_A * T_PAD, 1), 0) < T_PAD
    kall_ref[...] = jnp.zeros_like(kall_ref)
    vall_ref[...] = jnp.zeros_like(vall_ref)

    for bb in range(GB):
        r0 = bb * T_PAD
        kall_ref[0:WINDOW] = kbuf_ref[bb]
        vall_ref[0:WINDOW] = vbuf_ref[bb]
        kall_ref[WINDOW:WINDOW + T_PAD] = p[r0:r0 + T_PAD, C_KA:C_KA + KV_A]
        vall_ref[WINDOW:WINDOW + T_PAD] = p[r0:r0 + T_PAD, C_VA:C_VA + KV_A]
        kall, vall = kall_ref[...], vall_ref[...]
        for kh in range(KVH_A):
            q = jnp.concatenate([p[r0:r0 + T_PAD, C_QA + (kh * GQ_A + g) * HD_A:C_QA + (kh * GQ_A + g + 1) * HD_A]
                                 for g in range(GQ_A)], axis=0)
            s = _mm_nt(q, kall[:, kh * HD_A:(kh + 1) * HD_A]) * (HD_A ** -0.5) + bias_ref[kh]
            sinkcol = jnp.where(half, scal_ref[l, kh * GQ_A], scal_ref[l, kh * GQ_A + 1])
            o = _softmax_sink_pv(s, sinkcol, vall[:, kh * HD_A:(kh + 1) * HD_A])
            for g in range(GQ_A):
                c0 = (kh * GQ_A + g) * HD_A
                mix_ref[r0:r0 + T_PAD, c0:c0 + HD_A] = o[g * T_PAD:(g + 1) * T_PAD]

    kd_t = (p[:, C_KB:C_KB + H_B * DK_B] * kdec_ref[...]).T
    qd = p[:, C_QB:C_QB + H_B * DK_B] * qdec_ref[...]
    for h in range(H_B):
        q = p[:, C_QB + h * DK_B:C_QB + (h + 1) * DK_B]
        k = p[:, C_KB + h * DK_B:C_KB + (h + 1) * DK_B] * (DK_B ** -0.5)
        v = p[:, C_VB + h * DV_B:C_VB + (h + 1) * DV_B]
        a = _mm_nt(q, k) * dblk_ref[h]
        o_intra = _mm(a, v)
        for bb in range(GB):
            r0 = bb * T_PAD
            s0 = rs_ref[bb, h]
            oint_ref[r0:r0 + T_PAD, 0:DV_B] = _mm(qd[r0:r0 + T_PAD, h * DK_B:(h + 1) * DK_B], s0)
            rso_ref[bb, h] = (s0 * scal_ref[l, H_A + h]
                              + _mm(kd_t[h * DK_B:(h + 1) * DK_B] * _col_mask(BLK, T_PAD, bb), v))
        o = o_intra + oint_ref[:, 0:DV_B]
        c0 = H_A * HD_A + h * DV_B
        mix_ref[:, c0:c0 + DV_B] = _group_norm_gate(o, p[:, C_GB + h * DV_B:C_GB + (h + 1) * DV_B])

    o_diag, qe, k2, ebl = _hgrn_front(p, lb_ref[0], padmask_ref[...], tril_ref[...], sel_ref[...], onesbd_ref[...],
                                      b_ref, q_ref, T_PAD)
    k2_t, ebl_t = k2.T, ebl.T
    v = p[:, C_IC:C_IC + HC]
    for bb in range(GB):
        r0 = bb * T_PAD
        k2m = k2_t * _col_mask(BLK, T_PAD, bb)
        for h in range(H_C):
            s0 = hs_ref[bb, h]
            oint_ref[r0:r0 + T_PAD, h * DV_C:(h + 1) * DV_C] = _mm(qe[r0:r0 + T_PAD, h * DK_C:(h + 1) * DK_C], s0)
            hso_ref[bb, h] = (ebl_t[h * DK_C:(h + 1) * DK_C, r0:r0 + 1] * s0
                              + _mm(k2m[h * DK_C:(h + 1) * DK_C], v[:, h * DV_C:(h + 1) * DV_C]))
    o = oint_ref[...] + o_diag
    c0 = H_A * HD_A + H_B * DV_B
    mix_ref[:, c0:c0 + HC] = _hgrn_finish(o, p, gain_ref[0], bdf_ref[...])


def _mixer_sample(proj8, scal, kbuf, vbuf, rstate, hstate, lb, gain, l, bs, tabs, htab):
    nb = bs // GB
    full = lambda a: pl.BlockSpec(a.shape, (lambda i: (0, 0)) if a.ndim == 2 else (lambda i: (0, 0, 0)))
    layer3 = lambda i: (l, 0, 0)
    return pl.pallas_call(
        functools.partial(_mixer_sample_kernel, l=l),
        grid=(nb,),
        in_specs=[pl.BlockSpec(memory_space=pltpu.SMEM),
                  pl.BlockSpec((BLK, D_PROJ), lambda i: (i, 0)),
                  pl.BlockSpec((GB, WINDOW, KV_A), lambda i: (l * nb + i, 0, 0)),
                  pl.BlockSpec((GB, WINDOW, KV_A), lambda i: (l * nb + i, 0, 0)),
                  pl.BlockSpec((GB, H_B, DK_B, DV_B), lambda i: (l * nb + i, 0, 0, 0)),
                  pl.BlockSpec((GB, H_C, DK_C, DV_C), lambda i: (l * nb + i, 0, 0, 0)),
                  full(tabs["bias"]), full(tabs["dblk"]), full(tabs["qdec"]), full(tabs["kdec"]),
                  pl.BlockSpec((1, 1, HC), layer3), pl.BlockSpec((1, 1, HC), layer3), full(tabs["padmask"]),
                  full(htab["tril"]), full(htab["sel"]), full(htab["onesbd"]), full(htab["bdf"])],
        out_specs=[pl.BlockSpec((BLK, D_MODEL), lambda i: (i, 0)),
                   pl.BlockSpec((GB, H_B, DK_B, DV_B), lambda i: (i, 0, 0, 0)),
                   pl.BlockSpec((GB, H_C, DK_C, DV_C), lambda i: (i, 0, 0, 0))],
        out_shape=[jax.ShapeDtypeStruct((bs * T_PAD, D_MODEL), F32),
                   jax.ShapeDtypeStruct((bs, H_B, DK_B, DV_B), F32),
                   jax.ShapeDtypeStruct((bs, H_C, DK_C, DV_C), F32)],
        scratch_shapes=[pltpu.VMEM((2 * WINDOW, KV_A), F32), pltpu.VMEM((2 * WINDOW, KV_A), F32),
                        pltpu.VMEM((BLK, HC), F32), pltpu.VMEM((BLK, HC), F32), pltpu.VMEM((BLK, HC), F32)],
        compiler_params=_cparams("parallel"),
        name="mixer_sample",
    )(scal, proj8, kbuf, vbuf, rstate, hstate, tabs["bias"], tabs["dblk"], tabs["qdec"], tabs["kdec"],
      lb, gain, tabs["padmask"], htab["tril"], htab["sel"], htab["onesbd"], htab["bdf"])


def _attend_heads(q, mk_heads, mv_heads):
    outs = []
    for h in range(XH):
        s = _mm_nt(q[:, h * XHD:(h + 1) * XHD], mk_heads[h]) * (XHD ** -0.5)
        e = jnp.exp(s - jnp.max(s, axis=-1, keepdims=True))
        outs.append(_mm(e, mv_heads[h]) / jnp.sum(e, axis=-1, keepdims=True))
    return jnp.concatenate(outs, axis=-1)


def _attend_memory(q, mk, mv):
    return _attend_heads(q, [mk[:, h * XHD:(h + 1) * XHD] for h in range(XH)],
                         [mv[:, h * XHD:(h + 1) * XHD] for h in range(XH)])


def _xattn_prompt_kernel(x_ref, g_ref, wq_ref, wo_ref, mk_ref, mv_ref, o_ref):
    x = x_ref[...]
    q = jnp.dot(_rms(x, g_ref[0]).astype(BF16), wq_ref[0], preferred_element_type=F32)
    o = _attend_memory(q, mk_ref[...], mv_ref[...])
    o_ref[...] = x + jnp.dot(o.astype(BF16), wo_ref[0], preferred_element_type=F32)


def _xattn_prompt(x, g, wq, wo, mk, mv, l, bp, t_len):
    tq = _pick(t_len, (512, 256, 128))
    nt = t_len // tq
    dq = XH * XHD
    layer3 = lambda b, t: (l, 0, 0)
    return pl.pallas_call(
        _xattn_prompt_kernel,
        grid=(bp, nt),
        in_specs=[pl.BlockSpec((tq, D_MODEL), lambda b, t: (b * nt + t, 0)),
                  pl.BlockSpec((1, 1, D_MODEL), layer3),
                  pl.BlockSpec((1, D_MODEL, dq), layer3),
                  pl.BlockSpec((1, dq, D_MODEL), layer3),
                  pl.BlockSpec((N_MEM, dq), lambda b, t: (b, 0)),
                  pl.BlockSpec((N_MEM, dq), lambda b, t: (b, 0))],
        out_specs=pl.BlockSpec((tq, D_MODEL), lambda b, t: (b * nt + t, 0)),
        out_shape=jax.ShapeDtypeStruct(x.shape, F32),
        input_output_aliases={0: 0},
        compiler_params=_cparams("parallel", "parallel"),
        name="xattn_prompt",
    )(x, g, wq, wo, mk, mv)


XB = 4


def _xattn_sample_kernel(q_ref, mk_ref, mv_ref, o_ref):
    for bb in range(XB):
        r0 = bb * T_PAD
        o_ref[r0:r0 + T_PAD] = _attend_heads(q_ref[r0:r0 + T_PAD],
                                             [mk_ref[bb, :, h, :] for h in range(XH)],
                                             [mv_ref[bb, :, h, :] for h in range(XH)])


def _xattn_sample(q8, mk, mv, l, bs):
    dq = XH * XHD
    nb = bs // XB
    return pl.pallas_call(
        _xattn_sample_kernel,
        grid=(nb,),
        in_specs=[pl.BlockSpec((XB * T_PAD, dq), lambda i: (i, 0)),
                  pl.BlockSpec((XB, N_MEM, XH, XHD), lambda i: (l * nb + i, 0, 0, 0)),
                  pl.BlockSpec((XB, N_MEM, XH, XHD), lambda i: (l * nb + i, 0, 0, 0))],
        out_specs=pl.BlockSpec((XB * T_PAD, dq), lambda i: (i, 0)),
        out_shape=jax.ShapeDtypeStruct((bs * T_PAD, dq), F32),
        compiler_params=_cparams("parallel"),
        name="xattn_sample",
    )(q8, mk, mv)


def _ffn_kernel(be_ref, nused_ref, x_ref, g_ref, w1_ref, w3_ref, w2_ref, o_ref, xn_ref, *, dense):
    i, j = pl.program_id(0), pl.program_id(1)

    @pl.when(j == 0)
    def _():
        x = x_ref[...]
        if dense:
            xn_ref[...] = _rms(x, g_ref[0]).astype(BF16)
            o_ref[...] = x
        else:
            xn_ref[...] = x.astype(BF16)
            o_ref[...] = jnp.zeros_like(o_ref)

    @pl.when(i < nused_ref[0])
    def _():
        xn = xn_ref[...]
        h1 = jnp.dot(xn, w1_ref[0], preferred_element_type=F32)
        h3 = jnp.dot(xn, w3_ref[0], preferred_element_type=F32)
        o_ref[...] += jnp.dot((_silu(h1) * h3).astype(BF16), w2_ref[0], preferred_element_type=F32)


def _ffn(x, g, l, w1, w3, w2, block_e, nused, tm, dense):
    rows = x.shape[0]
    dff = w1.shape[2]
    tf = _pick(dff, (512, 256, 128))
    nf = dff // tf

    def xmap(i, j, be, nu):
        return (jnp.minimum(i, nu[0] - 1), 0)

    def w13map(i, j, be, nu):
        return (be[jnp.minimum(i, nu[0] - 1)], 0, jnp.where(i < nu[0], j, nf - 1))

    def w2map(i, j, be, nu):
        return (be[jnp.minimum(i, nu[0] - 1)], jnp.where(i < nu[0], j, nf - 1), 0)

    grid_spec = pltpu.PrefetchScalarGridSpec(
        num_scalar_prefetch=2,
        grid=(rows // tm, nf),
        in_specs=[pl.BlockSpec((tm, D_MODEL), xmap),
                  pl.BlockSpec((1, 1, D_MODEL), lambda i, j, be, nu: (l, 0, 0)),
                  pl.BlockSpec((1, D_MODEL, tf), w13map),
                  pl.BlockSpec((1, D_MODEL, tf), w13map),
                  pl.BlockSpec((1, tf, D_MODEL), w2map)],
        out_specs=pl.BlockSpec((tm, D_MODEL), lambda i, j, be, nu: (i, 0)),
        scratch_shapes=[pltpu.VMEM((tm, D_MODEL), BF16)],
    )
    return pl.pallas_call(
        functools.partial(_ffn_kernel, dense=dense),
        grid_spec=grid_spec,
        out_shape=jax.ShapeDtypeStruct((rows, D_MODEL), F32),
        input_output_aliases={2: 0} if dense else {},
        compiler_params=_cparams("arbitrary", "arbitrary"),
        name="ffn_dense" if dense else "ffn_experts",
    )(block_e, nused, x, g, w1, w3, w2)


def _router_kernel(x_ref, g_ref, wr_ref, h_ref, idx_ref, gate_ref):
    h = _rms(x_ref[...], g_ref[0])
    h_ref[...] = h
    logits = _mm_hi(h, wr_ref[...])
    lane = lax.broadcasted_iota(I32, logits.shape, 1)
    logits = jnp.where(lane < N_EXPERTS, logits, NEG_INF)
    m1 = jnp.max(logits, axis=-1, keepdims=True)
    i1 = jnp.min(jnp.where(logits == m1, lane, LANES), axis=-1, keepdims=True)
    rest = jnp.where(lane == i1, NEG_INF, logits)
    m2 = jnp.max(rest, axis=-1, keepdims=True)
    i2 = jnp.min(jnp.where(rest == m2, lane, LANES), axis=-1, keepdims=True)
    e2 = jnp.exp(m2 - m1)
    den = 1.0 + e2
    idx_ref[...] = jnp.where(lane == 0, i1, jnp.where(lane == 1, i2, 0))
    gate_ref[...] = jnp.where(lane == 0, 1.0 / den, jnp.where(lane == 1, e2 / den, 0.0))


def _router(x, g, l, w_router):
    n = x.shape[0]
    tm = _pick(n, ROW_TILES)
    wr = jnp.pad(w_router, ((0, 0), (0, LANES - N_EXPERTS)))
    return pl.pallas_call(
        _router_kernel,
        grid=(n // tm,),
        in_specs=[pl.BlockSpec((tm, D_MODEL), lambda i: (i, 0)),
                  pl.BlockSpec((1, 1, D_MODEL), lambda i: (l, 0, 0)),
                  pl.BlockSpec((D_MODEL, LANES), lambda i: (0, 0))],
        out_specs=[pl.BlockSpec((tm, D_MODEL), lambda i: (i, 0)),
                   pl.BlockSpec((tm, LANES), lambda i: (i, 0)),
                   pl.BlockSpec((tm, LANES), lambda i: (i, 0))],
        out_shape=[jax.ShapeDtypeStruct((n, D_MODEL), F32),
                   jax.ShapeDtypeStruct((n, LANES), I32),
                   jax.ShapeDtypeStruct((n, LANES), F32)],
        compiler_params=_cparams("parallel"),
        name="router",
    )(x, g, wr)


GATHER_UNROLL = 8


def _gather_rows_kernel(tok_ref, src_ref, o_ref, sem, *, tg):
    base = pl.program_id(0) * tg

    def issue(r, c):
        pltpu.make_async_copy(src_ref.at[pl.ds(tok_ref[base + r], 1)], o_ref.at[pl.ds(r, 1)], sem).start()
        return c

    lax.fori_loop(0, tg, issue, 0, unroll=GATHER_UNROLL)
    pltpu.make_async_copy(src_ref.at[pl.ds(0, tg)], o_ref, sem).wait()


def _gather_rows(src, tok, tg):
    rows = tok.shape[0]
    grid_spec = pltpu.PrefetchScalarGridSpec(
        num_scalar_prefetch=1,
        grid=(rows // tg,),
        in_specs=[pl.BlockSpec(memory_space=pl.ANY)],
        out_specs=pl.BlockSpec((tg, D_MODEL), lambda i, tok: (i, 0)),
        scratch_shapes=[pltpu.SemaphoreType.DMA],
    )
    return pl.pallas_call(
        functools.partial(_gather_rows_kernel, tg=tg),
        grid_spec=grid_spec,
        out_shape=jax.ShapeDtypeStruct((rows, D_MODEL), F32),
        compiler_params=_cparams("arbitrary"),
        name="moe_gather",
    )(tok, src)


def _expert_ffn_kernel(be_ref, nused_ref, tok_ref, h_ref, w1_ref, w3_ref, w2_ref, o_ref, xbuf_ref, xn_ref, sem,
                       *, tm, nf, nchunk):
    i, j = pl.program_id(0), pl.program_id(1)
    nu = nused_ref[0]
    slot = i % 2
    chunk = tm // nchunk

    def start_gather(blk, slot_, first_chunk, count):
        base = blk * tm

        def issue(k, c):
            r = first_chunk * chunk + k
            pltpu.make_async_copy(h_ref.at[pl.ds(tok_ref[base + r], 1)], xbuf_ref.at[slot_, pl.ds(r, 1)],
                                  sem.at[slot_]).start()
            return c

        lax.fori_loop(0, count * chunk, issue, 0, unroll=GATHER_UNROLL)

    first = nchunk - (nf - 1) if nchunk > 1 else 1

    if nchunk > 1:
        @pl.when((j > 0) & (i + 1 < nu))
        def _():
            start_gather(i + 1, 1 - slot, first - 1 + j, 1)

    @pl.when(j == 0)
    def _():
        @pl.when((i == 0) & (nu > 0))
        def _():
            start_gather(0, 0, 0, nchunk)

        @pl.when(i + 1 < nu)
        def _():
            start_gather(i + 1, 1 - slot, 0, first)

        @pl.when(i < nu)
        def _():
            pltpu.make_async_copy(h_ref.at[pl.ds(0, tm)], xbuf_ref.at[slot], sem.at[slot]).wait()
            xn_ref[...] = xbuf_ref[slot].astype(BF16)

        o_ref[...] = jnp.zeros_like(o_ref)

    @pl.when(i < nu)
    def _():
        xn = xn_ref[...]
        h1 = jnp.dot(xn, w1_ref[0].astype(BF16), preferred_element_type=F32)
        h3 = jnp.dot(xn, w3_ref[0].astype(BF16), preferred_element_type=F32)
        o_ref[...] += jnp.dot((_silu(h1) * h3).astype(BF16), w2_ref[0].astype(BF16), preferred_element_type=F32)


def _expert_ffn(h, tok, w1, w3, w2, block_e, nused, tm):
    rows = tok.shape[0]
    dff = w1.shape[2]
    tf = _pick(dff, (512, 256, 128))
    nf = dff // tf
    nchunk = nf + 1 if tm % ((nf + 1) * GATHER_UNROLL) == 0 else 1

    def w13map(i, j, be, nu, tok):
        return (be[jnp.minimum(i, nu[0] - 1)], 0, jnp.where(i < nu[0], j, nf - 1))

    def w2map(i, j, be, nu, tok):
        return (be[jnp.minimum(i, nu[0] - 1)], jnp.where(i < nu[0], j, nf - 1), 0)

    grid_spec = pltpu.PrefetchScalarGridSpec(
        num_scalar_prefetch=3,
        grid=(rows // tm, nf),
        in_specs=[pl.BlockSpec(memory_space=pl.ANY),
                  pl.BlockSpec((1, D_MODEL, tf), w13map),
                  pl.BlockSpec((1, D_MODEL, tf), w13map),
                  pl.BlockSpec((1, tf, D_MODEL), w2map)],
        out_specs=pl.BlockSpec((tm, D_MODEL), lambda i, j, be, nu, tok: (i, 0)),
        scratch_shapes=[pltpu.VMEM((2, tm, D_MODEL), F32), pltpu.VMEM((tm, D_MODEL), BF16),
                        pltpu.SemaphoreType.DMA((2,))],
    )
    return pl.pallas_call(
        functools.partial(_expert_ffn_kernel, tm=tm, nf=nf, nchunk=nchunk),
        grid_spec=grid_spec,
        out_shape=jax.ShapeDtypeStruct((rows, D_MODEL), F32),
        compiler_params=_cparams("arbitrary", "arbitrary"),
        name="ffn_experts",
    )(block_e, nused, tok, h, w1, w3, w2)


def _combine_kernel(pos_ref, x_ref, gate_ref, y_ref, o_ref, buf_ref, sem, *, tc):
    base = pl.program_id(0) * tc

    def issue(r, c):
        for k in range(TOP_K):
            pltpu.make_async_copy(y_ref.at[pl.ds(pos_ref[(base + r) * TOP_K + k], 1)],
                                  buf_ref.at[k, pl.ds(r, 1)], sem).start()
        return c

    lax.fori_loop(0, tc, issue, 0, unroll=GATHER_UNROLL)
    for k in range(TOP_K):
        pltpu.make_async_copy(y_ref.at[pl.ds(0, tc)], buf_ref.at[k], sem).wait()
    gate = gate_ref[...]
    o_ref[...] = x_ref[...] + (buf_ref[0] * gate[:, 0:1] + buf_ref[1] * gate[:, 1:2])


def _combine(x, gate, y, pos, tc):
    n = x.shape[0]
    grid_spec = pltpu.PrefetchScalarGridSpec(
        num_scalar_prefetch=1,
        grid=(n // tc,),
        in_specs=[pl.BlockSpec((tc, D_MODEL), lambda i, pos: (i, 0)),
                  pl.BlockSpec((tc, LANES), lambda i, pos: (i, 0)),
                  pl.BlockSpec(memory_space=pl.ANY)],
        out_specs=pl.BlockSpec((tc, D_MODEL), lambda i, pos: (i, 0)),
        scratch_shapes=[pltpu.VMEM((TOP_K, tc, D_MODEL), F32), pltpu.SemaphoreType.DMA],
    )
    return pl.pallas_call(
        functools.partial(_combine_kernel, tc=tc),
        grid_spec=grid_spec,
        out_shape=jax.ShapeDtypeStruct(x.shape, F32),
        input_output_aliases={1: 0},
        compiler_params=_cparams("arbitrary"),
        name="moe_combine",
    )(pos, x, gate, y)


def _moe(x, g, l, w_router, w1, w3, w2, e0):
    n = x.shape[0]
    a = n * TOP_K
    tm = _pick(a, (1024,) + ROW_TILES)
    h, idx, gate = _router(x, g, l, w_router)
    e_flat = idx[:, :TOP_K].reshape(a)
    onehot = (e_flat[:, None] == jnp.arange(N_EXPERTS, dtype=I32)[None, :]).astype(I32)
    csum = jnp.cumsum(onehot, axis=0)
    counts = csum[-1]
    rank = jnp.sum((csum - onehot) * onehot, axis=1)
    padded = (counts + tm - 1) // tm * tm
    pend = jnp.cumsum(padded)
    pstart = pend - padded
    dest = (pstart[e_flat] + rank).astype(I32)
    nb = a // tm + N_EXPERTS
    tok = jnp.zeros((nb * tm,), I32).at[dest].set(jnp.repeat(jnp.arange(n, dtype=I32), TOP_K))
    block_start = jnp.arange(nb, dtype=I32) * tm
    block_e = jnp.minimum(jnp.sum((pend[None, :] <= block_start[:, None]).astype(I32), axis=1), N_EXPERTS - 1)
    nused = (pend[-1:] // tm).astype(I32)
    y = _expert_ffn(h, tok, w1, w3, w2, (block_e + e0).astype(I32), nused, tm)
    return _combine(x, gate, y, dest, _pick(n, (256, 128, 64, 32, 16, 8)))


def kernel(x_prompt, x_sample, cache_swa_k, cache_swa_v, state_ret, state_hgrn, cache_mem_k, cache_mem_v, mem_prompt, norm_mix, w_in, swa_sinks, hgrn_lb_logits, hgrn_norm, w_out, norm_xattn, norm_mem, wx_q, wx_k, wx_v, wx_o, norm_ffn, ffn_w1, ffn_w3, ffn_w2, moe_router, moe_w1, moe_w3, moe_w2, final_norm):
    bp, t_len, _ = x_prompt.shape
    bs, ts, _ = x_sample.shape
    depth = w_in.shape[0]
    n_p, n_s = bp * t_len, bs * ts
    n = n_p + n_s
    assert t_len % BLK == 0 and bs % GB == 0 and bs % XB == 0 and ts <= T_PAD
    assert cache_swa_k.shape[2] == WINDOW and ts < WINDOW
    dq = XH * XHD

    x = jnp.concatenate([x_prompt.reshape(n_p, D_MODEL), x_sample.reshape(n_s, D_MODEL)], axis=0)
    ptab, stab = _prompt_tables(), _sample_tables(ts)
    htab_p, htab_s = _hgrn_tables(SUB_P), _hgrn_tables(T_PAD)
    lb_p = jax.nn.softmax(hgrn_lb_logits.astype(F32), axis=0)
    lower_bounds = (jnp.cumsum(lb_p, axis=0) - lb_p[0]).reshape(depth, 1, HC)
    gains = hgrn_norm.astype(F32).reshape(depth, 1, HC)
    sinks = swa_sinks.astype(F32)
    scal_p = jnp.concatenate([sinks, jnp.broadcast_to(jnp.asarray(ptab["cdec"], F32), (depth, H_B))], axis=1)
    scal_s = jnp.concatenate([sinks, jnp.broadcast_to(jnp.asarray(stab["cdec"], F32), (depth, H_B))], axis=1)
    mem = mem_prompt.reshape(bp * N_MEM, D_MODEL)

    bf = lambda w: w.astype(BF16)
    w_in_b, w_out_b = bf(w_in), bf(w_out)
    wq_b, wk_b, wv_b, wo_b = bf(wx_q), bf(wx_k), bf(wx_v), bf(wx_o)
    f1_b, f3_b, f2_b = bf(ffn_w1), bf(ffn_w3), bf(ffn_w2)
    m1_b = moe_w1.astype(F32).reshape((-1,) + moe_w1.shape[2:])
    m3_b = moe_w3.astype(F32).reshape((-1,) + moe_w3.shape[2:])
    m2_b = moe_w2.astype(F32).reshape((-1,) + moe_w2.shape[2:])
    vec3 = lambda v: v.astype(F32).reshape(depth, 1, D_MODEL)
    n_mix, n_xat, n_mem, n_ffn = vec3(norm_mix), vec3(norm_xattn), vec3(norm_mem), vec3(norm_ffn)

    kbuf = cache_swa_k.astype(F32).reshape(depth * bs, WINDOW, KV_A)
    vbuf = cache_swa_v.astype(F32).reshape(depth * bs, WINDOW, KV_A)
    rstate = state_ret.astype(F32).reshape((depth * bs,) + state_ret.shape[2:])
    hstate = state_hgrn.astype(F32).reshape((depth * bs,) + state_hgrn.shape[2:])
    cmk = cache_mem_k.astype(F32).reshape(depth * bs, N_MEM, XH, XHD)
    cmv = cache_mem_v.astype(F32).reshape(depth * bs, N_MEM, XH, XHD)

    def pad_tokens(a):
        return jnp.pad(a.reshape(bs, ts, -1), ((0, 0), (0, T_PAD - ts), (0, 0))).reshape(bs * T_PAD, -1)

    def unpad_tokens(a):
        return a.reshape(bs, T_PAD, -1)[:, :ts].reshape(bs * ts, -1)

    outs = {k: [] for k in ("pk", "pv", "pr", "pc", "pmk", "pmv", "sk", "sv", "sr", "sc")}
    for l in range(depth):
        proj = _rms_matmul(x, n_mix, w_in_b, l)
        mix_p, kkeep, vkeep, ret_p, hg_p = _mixer_prompt(proj, scal_p, lower_bounds, gains, l, bp, t_len, ptab, htab_p)
        proj_s = lax.slice_in_dim(proj, n_p, n, axis=0)
        mix_s, ret_s, hg_s = _mixer_sample(pad_tokens(proj_s), scal_s, kbuf, vbuf, rstate, hstate,
                                           lower_bounds, gains, l, bs, stab, htab_s)
        x = _matmul_res(mix_p, w_out_b, l, x, 0)
        x = _matmul_res(unpad_tokens(mix_s), w_out_b, l, x, n_p)

        outs["pk"].append(kkeep.reshape(bp, WINDOW, KVH_A, HD_A))
        outs["pv"].append(vkeep.reshape(bp, WINDOW, KVH_A, HD_A))
        outs["pr"].append(ret_p)
        outs["pc"].append(jnp.stack([hg_p[:, h * DK_C:(h + 1) * DK_C, h * DV_C:(h + 1) * DV_C]
                                     for h in range(H_C)], axis=1))
        new_k = proj_s[:, C_KA:C_KA + KV_A].reshape(bs, ts, KVH_A, HD_A)
        new_v = proj_s[:, C_VA:C_VA + KV_A].reshape(bs, ts, KVH_A, HD_A)
        outs["sk"].append(jnp.concatenate([cache_swa_k[l][:, ts:].astype(F32), new_k], axis=1))
        outs["sv"].append(jnp.concatenate([cache_swa_v[l][:, ts:].astype(F32), new_v], axis=1))
        outs["sr"].append(ret_s)
        outs["sc"].append(hg_s)

        mk = _rms_matmul(mem, n_mem, wk_b, l)
        mv = _rms_matmul(mem, n_mem, wv_b, l)
        outs["pmk"].append(mk.reshape(bp, N_MEM, XH, XHD))
        outs["pmv"].append(mv.reshape(bp, N_MEM, XH, XHD))
        x = _xattn_prompt(x, n_xat, wq_b, wo_b, mk, mv, l, bp, t_len)
        q_s = _rms_matmul(x, n_xat, wq_b, l, row_off=n_p, rows=n_s)
        o_s = _xattn_sample(pad_tokens(q_s), cmk, cmv, l, bs)
        x = _matmul_res(unpad_tokens(o_s), wo_b, l, x, n_p)

        e = l // 2
        if l % 2 == 0:
            tm = _pick(n, (1536, 768) + ROW_TILES)
            nb = n // tm
            x = _ffn(x, n_ffn, l, f1_b, f3_b, f2_b, jnp.full((nb,), e, I32), jnp.full((1,), nb, I32), tm, dense=True)
        else:
            x = _moe(x, n_ffn, l, moe_router[e], m1_b, m3_b, m2_b, e * N_EXPERTS)

    y_p = _rmsnorm_rows(x, final_norm, 0, n_p).reshape(bp, t_len, D_MODEL)
    y_s = _rmsnorm_rows(x, final_norm, n_p, n_s).reshape(bs, ts, D_MODEL)
    st = lambda k: jnp.stack(outs[k])
    return (y_p, y_s, st("pk"), st("pv"), st("pr"), st("pc"), st("pmk"), st("pmv"),
            st("sk"), st("sv"), st("sr"), st("sc"))
```

```python
import functools

import numpy as np
import jax
import jax.numpy as jnp
from jax import lax
from jax.experimental import pallas as pl
from jax.experimental.pallas import tpu as pltpu

F32, BF16, I32 = jnp.float32, jnp.bfloat16, jnp.int32
HIGHEST = lax.Precision.HIGHEST
NEG_INF = float("-inf")

D_MODEL = 1024
WINDOW = 128
H_A, KVH_A, GQ_A, HD_A = 4, 2, 2, 64
ALIBI_MAX_EXP = 8.0
H_B, DK_B, DV_B = 4, 64, 128
RET_CHUNK = 128
H_C, DK_C, DV_C = 4, 64, 64
N_MEM, XH, XHD = 256, 4, 128
N_EXPERTS, TOP_K = 8, 2
EPS = 1e-6
D_PROJ = 3072
C_QA, C_KA, C_VA = 0, 256, 384
C_QB, C_KB, C_VB, C_GB = 512, 768, 1024, 1536
C_FC, C_QC, C_IC, C_GC = 2048, 2304, 2560, 2816
KV_A = KVH_A * HD_A
HC = H_C * DK_C

BLK = 128
SUB_P = 16
T_PAD = 8
GB = BLK // T_PAD
LANES = 128
VMEM_LIMIT = 56 * 1024 * 1024
ROW_TILES = (512, 256, 128, 64, 32, 16, 8)


def _pick(n, cands):
    for c in cands:
        if n % c == 0:
            return c
    raise ValueError(f"no tile for {n}")


def _row_tile(rows, row_off):
    return _pick(int(np.gcd(rows, row_off)) if row_off else rows, ROW_TILES)


def _cparams(*sem):
    return pltpu.CompilerParams(dimension_semantics=sem, vmem_limit_bytes=VMEM_LIMIT)


def _mm(a, b):
    return jnp.dot(a.astype(BF16), b.astype(BF16), preferred_element_type=F32)


def _mm_nt(a, b):
    return lax.dot_general(a.astype(BF16), b.astype(BF16), (((1,), (1,)), ((), ())),
                           preferred_element_type=F32)


def _mm_hi(a, b):
    return jnp.dot(a, b, precision=HIGHEST, preferred_element_type=F32)


def _sigmoid(x):
    return 1.0 / (1.0 + jnp.exp(-x))


def _silu(x):
    return x * _sigmoid(x)


def _rms(x, g):
    return x * lax.rsqrt(jnp.mean(x * x, axis=-1, keepdims=True) + EPS) * g


def _rms_matmul_kernel(x_ref, g_ref, w_ref, o_ref, xn_ref):
    @pl.when(pl.program_id(1) == 0)
    def _():
        xn_ref[...] = _rms(x_ref[...], g_ref[0]).astype(BF16)

    o_ref[...] = jnp.dot(xn_ref[...], w_ref[0], preferred_element_type=F32)


def _rms_matmul(x, g, w, l, row_off=0, rows=None):
    n, k = x.shape
    m = w.shape[2]
    rows = n if rows is None else rows
    tm = _row_tile(rows, row_off)
    tn = _pick(m, (3072, 1536, 1024, 512, 256, 128))
    off = row_off // tm
    return pl.pallas_call(
        _rms_matmul_kernel,
        grid=(rows // tm, m // tn),
        in_specs=[pl.BlockSpec((tm, k), lambda i, j: (i + off, 0)),
                  pl.BlockSpec((1, 1, k), lambda i, j: (l, 0, 0)),
                  pl.BlockSpec((1, k, tn), lambda i, j: (l, 0, j))],
        out_specs=pl.BlockSpec((tm, tn), lambda i, j: (i, j)),
        out_shape=jax.ShapeDtypeStruct((rows, m), F32),
        scratch_shapes=[pltpu.VMEM((tm, k), BF16)],
        compiler_params=_cparams("parallel", "arbitrary"),
        name="rms_matmul",
    )(x, g, w)


def _matmul_res_kernel(a_ref, w_ref, x_ref, o_ref):
    o_ref[...] = x_ref[...] + jnp.dot(a_ref[...].astype(BF16), w_ref[0], preferred_element_type=F32)


def _matmul_res(a, w, l, x, row_off):
    rows, k = a.shape
    m = w.shape[2]
    tm = _row_tile(rows, row_off)
    off = row_off // tm
    return pl.pallas_call(
        _matmul_res_kernel,
        grid=(rows // tm,),
        in_specs=[pl.BlockSpec((tm, k), lambda i: (i, 0)),
                  pl.BlockSpec((1, k, m), lambda i: (l, 0, 0)),
                  pl.BlockSpec((tm, m), lambda i: (i + off, 0))],
        out_specs=pl.BlockSpec((tm, m), lambda i: (i + off, 0)),
        out_shape=jax.ShapeDtypeStruct(x.shape, F32),
        input_output_aliases={2: 0},
        compiler_params=_cparams("parallel"),
        name="matmul_res",
    )(a, w, x)


def _rmsnorm_kernel(x_ref, g_ref, o_ref):
    o_ref[...] = _rms(x_ref[...], g_ref[...])


def _rmsnorm_rows(x, g, row_off, rows):
    n, k = x.shape
    tm = _row_tile(rows, row_off)
    off = row_off // tm
    return pl.pallas_call(
        _rmsnorm_kernel,
        grid=(rows // tm,),
        in_specs=[pl.BlockSpec((tm, k), lambda i: (i + off, 0)),
                  pl.BlockSpec((1, k), lambda i: (0, 0))],
        out_specs=pl.BlockSpec((tm, k), lambda i: (i, 0)),
        out_shape=jax.ShapeDtypeStruct((rows, k), F32),
        compiler_params=_cparams("parallel"),
        name="final_rmsnorm",
    )(x, g.reshape(1, k))


def _retention_gamma():
    return 1.0 - np.exp2(-5.0 - np.arange(H_B, dtype=np.float64))


def _alibi_slopes():
    return np.exp2(-(ALIBI_MAX_EXP / H_A) * np.arange(1, H_A + 1, dtype=np.float64))


def _head_cols(vals_hr, width):
    return np.repeat(vals_hr.T[:, :, None], width, axis=2).reshape(vals_hr.shape[1], -1)


def _hgrn_tables(sub):
    r = np.arange(BLK)
    same = (r[:, None] // sub) == (r[None, :] // sub)
    tril = (same & (r[None, :] <= r[:, None])).astype(np.float32)
    pr = np.arange(BLK * sub)
    sel = (r[:, None] == (pr // sub)[None, :]).astype(np.float32)
    hh = np.arange(HC) // DK_C
    bd = (hh[:, None] == hh[None, :]).astype(np.float32)
    return dict(tril=jnp.asarray(tril), sel=jnp.asarray(sel, dtype=BF16),
                onesbd=jnp.asarray(bd, dtype=BF16), bdf=jnp.asarray(bd))


def _prompt_tables():
    gam = _retention_gamma()
    i = np.arange(RET_CHUNK, dtype=np.float64)
    diff = i[:, None] - i[None, :]
    dmat = np.where(diff >= 0, gam[:, None, None] ** np.maximum(diff, 0.0), 0.0)
    qdec = _head_cols(gam[:, None] ** (i + 1.0), DK_B)
    kdec = _head_cols(gam[:, None] ** (RET_CHUNK - 1.0 - i), DK_B) * (DK_B ** -0.5)
    cdec = gam ** RET_CHUNK
    slopes = _alibi_slopes().reshape(KVH_A, GQ_A)
    qi = np.tile(np.arange(WINDOW), GQ_A)[:, None]
    qg = np.repeat(np.arange(GQ_A), WINDOW)[:, None]
    j = np.arange(2 * WINDOW)[None, :]
    rel = WINDOW + qi - j
    ok = (rel >= 0) & (rel < WINDOW)
    bias = np.stack([np.where(ok, -slopes[kh][qg] * rel, NEG_INF) for kh in range(KVH_A)])
    bias_first = np.where(j < WINDOW, NEG_INF, bias)
    return dict(dmat=jnp.asarray(dmat, dtype=F32), qdec=jnp.asarray(qdec, dtype=F32),
                kdec=jnp.asarray(kdec, dtype=F32), cdec=cdec,
                bias=jnp.asarray(np.stack([bias_first, bias]), dtype=F32))


def _sample_tables(t):
    gam = _retention_gamma()
    r = np.arange(BLK)
    ti, bi = r % T_PAD, r // T_PAD
    diff = (ti[:, None] - ti[None, :]).astype(np.float64)
    same = bi[:, None] == bi[None, :]
    dblk = np.where(same & (diff >= 0), gam[:, None, None] ** np.maximum(diff, 0.0), 0.0)
    qdec = _head_cols(gam[:, None] ** (ti + 1.0), DK_B)
    kdec = _head_cols(gam[:, None] ** (t - 1.0 - ti), DK_B) * (DK_B ** -0.5)
    cdec = gam ** t
    slopes = _alibi_slopes().reshape(KVH_A, GQ_A)
    qt = np.tile(np.arange(T_PAD), GQ_A)[:, None]
    qg = np.repeat(np.arange(GQ_A), T_PAD)[:, None]
    c = np.arange(2 * WINDOW)[None, :]
    rel = qt + WINDOW - c
    ok = (rel >= 0) & (rel < WINDOW) & (c < WINDOW + t)
    bias = np.stack([np.where(ok, -slopes[kh][qg] * rel, NEG_INF) for kh in range(KVH_A)])
    padmask = np.repeat((ti < t).astype(np.float32)[:, None], HC, axis=1)
    return dict(dblk=jnp.asarray(dblk, dtype=F32), qdec=jnp.asarray(qdec, dtype=F32),
                kdec=jnp.asarray(kdec, dtype=F32), cdec=cdec, bias=jnp.asarray(bias, dtype=F32),
                padmask=jnp.asarray(padmask))


def _tile_j(x, sub):
    nsub = x.shape[0] // sub
    y = jnp.broadcast_to(x.reshape(nsub, 1, sub, x.shape[1]), (nsub, sub, sub, x.shape[1]))
    return y.reshape(nsub * sub * sub, x.shape[1])


def _row_bcast(ref, r, sub):
    return jnp.broadcast_to(ref[r:r + 1, :], (sub, ref.shape[1]))


def _hgrn_front(p, lb, padmask, tril, sel, onesbd, b_ref, q_ref, sub):
    fc, qc, ic = p[:, C_FC:C_FC + HC], p[:, C_QC:C_QC + HC], p[:, C_IC:C_IC + HC]
    f = lb + (1.0 - lb) * _sigmoid(fc)
    kk = 1.0 - f
    g = jnp.log(f)
    if padmask is not None:
        kk = kk * padmask
        g = g * padmask
    qq = _silu(qc)
    b = _mm_hi(tril, g)
    b_ref[...] = b
    q_ref[...] = qq
    jrow = lax.broadcasted_iota(I32, (sub, 1), 0)
    bi = jnp.concatenate([_row_bcast(b_ref, r, sub) for r in range(BLK)], axis=0)
    qi = jnp.concatenate([jnp.where(jrow <= r % sub, _row_bcast(q_ref, r, sub), 0.0) for r in range(BLK)], axis=0)
    w = jnp.exp(jnp.minimum(bi - _tile_j(b, sub), 0.0))
    wm = qi * _tile_j(kk, sub) * w
    ab = jnp.dot(wm.astype(BF16), onesbd, preferred_element_type=F32)
    cm = ab * _tile_j(ic, sub)
    o_diag = jnp.dot(sel, cm.astype(BF16), preferred_element_type=F32)
    blrep = jnp.concatenate([_row_bcast(b_ref, i * sub + sub - 1, sub) for i in range(BLK // sub)], axis=0)
    return o_diag, qq * jnp.exp(b), kk * jnp.exp(blrep - b), jnp.exp(blrep)


def _hgrn_finish(o, p, gain, bdf):
    ms = _mm_hi(o * o, bdf) * (1.0 / DV_C)
    return o * lax.rsqrt(ms + EPS) * gain * _silu(p[:, C_GC:C_GC + HC])


def _group_norm_gate(o, gate):
    c = o - jnp.mean(o, axis=-1, keepdims=True)
    return c * lax.rsqrt(jnp.mean(c * c, axis=-1, keepdims=True) + EPS) * _silu(gate)


def _softmax_sink_pv(s, sinkcol, v):
    m = jnp.maximum(jnp.max(s, axis=-1, keepdims=True), sinkcol)
    e = jnp.exp(s - m)
    den = jnp.sum(e, axis=-1, keepdims=True) + jnp.exp(sinkcol - m)
    return _mm(e, v) / den


def _col_mask(width, sub, idx):
    return (lax.broadcasted_iota(I32, (1, width), 1) // sub == idx).astype(F32)


def _mixer_prompt_kernel(scal_ref, p_ref, prev_ref, bias_ref, dmat_ref, qdec_ref, kdec_ref, lb_ref, gain_ref,
                         tril_ref, sel_ref, onesbd_ref, bdf_ref,
                         mix_ref, kkeep_ref, vkeep_ref, rets_ref, hgs_ref,
                         sret_ref, shg_ref, b_ref, q_ref, *, l):
    t = pl.program_id(1)

    @pl.when(t == 0)
    def _():
        sret_ref[...] = jnp.zeros_like(sret_ref)
        shg_ref[...] = jnp.zeros_like(shg_ref)

    p = p_ref[...]
    prev = prev_ref[...]
    half = lax.broadcasted_iota(I32, (GQ_A * BLK, 1), 0) < BLK

    for kh in range(KVH_A):
        q = jnp.concatenate([p[:, C_QA + (kh * GQ_A + g) * HD_A:C_QA + (kh * GQ_A + g + 1) * HD_A]
                             for g in range(GQ_A)], axis=0)
        kband = jnp.concatenate([prev[:, kh * HD_A:(kh + 1) * HD_A],
                                 p[:, C_KA + kh * HD_A:C_KA + (kh + 1) * HD_A]], axis=0)
        vband = jnp.concatenate([prev[:, KV_A + kh * HD_A:KV_A + (kh + 1) * HD_A],
                                 p[:, C_VA + kh * HD_A:C_VA + (kh + 1) * HD_A]], axis=0)
        s = _mm_nt(q, kband) * (HD_A ** -0.5) + bias_ref[0, kh]
        sinkcol = jnp.where(half, scal_ref[l, kh * GQ_A], scal_ref[l, kh * GQ_A + 1])
        o = _softmax_sink_pv(s, sinkcol, vband)
        for g in range(GQ_A):
            c0 = (kh * GQ_A + g) * HD_A
            mix_ref[:, c0:c0 + HD_A] = o[g * BLK:(g + 1) * BLK]

    kd_t = (p[:, C_KB:C_KB + H_B * DK_B] * kdec_ref[...]).T
    qd = p[:, C_QB:C_QB + H_B * DK_B] * qdec_ref[...]
    for h in range(H_B):
        q = p[:, C_QB + h * DK_B:C_QB + (h + 1) * DK_B]
        k = p[:, C_KB + h * DK_B:C_KB + (h + 1) * DK_B] * (DK_B ** -0.5)
        v = p[:, C_VB + h * DV_B:C_VB + (h + 1) * DV_B]
        a = _mm_nt(q, k) * dmat_ref[h]
        s0 = sret_ref[h]
        o = _mm(a, v) + _mm(qd[:, h * DK_B:(h + 1) * DK_B], s0)
        sret_ref[h] = s0 * scal_ref[l, H_A + h] + _mm(kd_t[h * DK_B:(h + 1) * DK_B], v)
        c0 = H_A * HD_A + h * DV_B
        mix_ref[:, c0:c0 + DV_B] = _group_norm_gate(o, p[:, C_GB + h * DV_B:C_GB + (h + 1) * DV_B])

    o_diag, qe, k2, ebl = _hgrn_front(p, lb_ref[0], None, tril_ref[...], sel_ref[...], onesbd_ref[...],
                                      b_ref, q_ref, SUB_P)
    k2_t, ebl_t = k2.T, ebl.T
    v = p[:, C_IC:C_IC + HC]
    bdf = bdf_ref[...]
    st = shg_ref[...]
    o_inter = []
    for i in range(BLK // SUB_P):
        o_inter.append(_mm(qe[i * SUB_P:(i + 1) * SUB_P], st))
        u = _mm(k2_t * _col_mask(BLK, SUB_P, i), v) * bdf
        st = ebl_t[:, i * SUB_P:i * SUB_P + 1] * st + u
    shg_ref[...] = st
    o = jnp.concatenate(o_inter, axis=0) + o_diag
    c0 = H_A * HD_A + H_B * DV_B
    mix_ref[:, c0:c0 + HC] = _hgrn_finish(o, p, gain_ref[0], bdf)

    @pl.when(t == pl.num_programs(1) - 1)
    def _():
        kkeep_ref[0] = p[:, C_KA:C_KA + KV_A]
        vkeep_ref[0] = p[:, C_VA:C_VA + KV_A]
        rets_ref[0] = sret_ref[...]
        hgs_ref[0] = shg_ref[...]


def _mixer_prompt(proj, scal, lb, gain, l, bp, t_len, tabs, htab):
    nb = t_len // BLK
    const2 = lambda b, t: (0, 0)
    const3 = lambda b, t: (0, 0, 0)
    layer3 = lambda b, t: (l, 0, 0)
    full = lambda a: pl.BlockSpec(a.shape, const2 if a.ndim == 2 else const3)
    return pl.pallas_call(
        functools.partial(_mixer_prompt_kernel, l=l),
        grid=(bp, nb),
        in_specs=[pl.BlockSpec(memory_space=pltpu.SMEM),
                  pl.BlockSpec((BLK, D_PROJ), lambda b, t: (b * nb + t, 0)),
                  pl.BlockSpec((BLK, 2 * KV_A), lambda b, t: (b * nb + jnp.maximum(t - 1, 0), 1)),
                  pl.BlockSpec((1, KVH_A, GQ_A * BLK, 2 * BLK), lambda b, t: (jnp.minimum(t, 1), 0, 0, 0)),
                  full(tabs["dmat"]), full(tabs["qdec"]), full(tabs["kdec"]),
                  pl.BlockSpec((1, 1, HC), layer3), pl.BlockSpec((1, 1, HC), layer3),
                  full(htab["tril"]), full(htab["sel"]), full(htab["onesbd"]), full(htab["bdf"])],
        out_specs=[pl.BlockSpec((BLK, D_MODEL), lambda b, t: (b * nb + t, 0)),
                   pl.BlockSpec((1, WINDOW, KV_A), lambda b, t: (b, 0, 0)),
                   pl.BlockSpec((1, WINDOW, KV_A), lambda b, t: (b, 0, 0)),
                   pl.BlockSpec((1, H_B, DK_B, DV_B), lambda b, t: (b, 0, 0, 0)),
                   pl.BlockSpec((1, HC, HC), lambda b, t: (b, 0, 0))],
        out_shape=[jax.ShapeDtypeStruct((bp * t_len, D_MODEL), F32),
                   jax.ShapeDtypeStruct((bp, WINDOW, KV_A), F32),
                   jax.ShapeDtypeStruct((bp, WINDOW, KV_A), F32),
                   jax.ShapeDtypeStruct((bp, H_B, DK_B, DV_B), F32),
                   jax.ShapeDtypeStruct((bp, HC, HC), F32)],
        scratch_shapes=[pltpu.VMEM((H_B, DK_B, DV_B), F32), pltpu.VMEM((HC, HC), F32),
                        pltpu.VMEM((BLK, HC), F32), pltpu.VMEM((BLK, HC), F32)],
        compiler_params=_cparams("parallel", "arbitrary"),
        name="mixer_prompt",
    )(scal, proj, proj, tabs["bias"], tabs["dmat"], tabs["qdec"], tabs["kdec"], lb, gain,
      htab["tril"], htab["sel"], htab["onesbd"], htab["bdf"])


def _mixer_sample_kernel(scal_ref, p_ref, kbuf_ref, vbuf_ref, rs_ref, hs_ref, bias_ref, dblk_ref, qdec_ref,
                         kdec_ref, lb_ref, gain_ref, padmask_ref, tril_ref, sel_ref, onesbd_ref, bdf_ref,
                         mix_ref, rso_ref, hso_ref, kall_ref, vall_ref, oint_ref, b_ref, q_ref, *, l):
    p = p_ref[...]
    half = lax.broadcasted_iota(I32, (GQ_A * T_PAD, 1), 0) < T_PAD
    kall_ref[...] = jnp.zeros_like(kall_ref)
    vall_ref[...] = jnp.zeros_like(vall_ref)

    for bb in range(GB):
        r0 = bb * T_PAD
        kall_ref[0:WINDOW] = kbuf_ref[bb]
        vall_ref[0:WINDOW] = vbuf_ref[bb]
        kall_ref[WINDOW:WINDOW + T_PAD] = p[r0:r0 + T_PAD, C_KA:C_KA + KV_A]
        vall_ref[WINDOW:WINDOW + T_PAD] = p[r0:r0 + T_PAD, C_VA:C_VA + KV_A]
        kall, vall = kall_ref[...], vall_ref[...]
        for kh in range(KVH_A):
            q = jnp.concatenate([p[r0:r0 + T_PAD, C_QA + (kh * GQ_A + g) * HD_A:C_QA + (kh * GQ_A + g + 1) * HD_A]
                                 for g in range(GQ_A)], axis=0)
            s = _mm_nt(q, kall[:, kh * HD_A:(kh + 1) * HD_A]) * (HD_A ** -0.5) + bias_ref[kh]
            sinkcol = jnp.where(half, scal_ref[l, kh * GQ_A], scal_ref[l, kh * GQ_A + 1])
            o = _softmax_sink_pv(s, sinkcol, vall[:, kh * HD_A:(kh + 1) * HD_A])
            for g in range(GQ_A):
                c0 = (kh * GQ_A + g) * HD_A
                mix_ref[r0:r0 + T_PAD, c0:c0 + HD_A] = o[g * T_PAD:(g + 1) * T_PAD]

    kd_t = (p[:, C_KB:C_KB + H_B * DK_B] * kdec_ref[...]).T
    qd = p[:, C_QB:C_QB + H_B * DK_B] * qdec_ref[...]
    for h in range(H_B):
        q = p[:, C_QB + h * DK_B:C_QB + (h + 1) * DK_B]
        k = p[:, C_KB + h * DK_B:C_KB + (h + 1) * DK_B] * (DK_B ** -0.5)
        v = p[:, C_VB + h * DV_B:C_VB + (h + 1) * DV_B]
        a = _mm_nt(q, k) * dblk_ref[h]
        o_intra = _mm(a, v)
        for bb in range(GB):
            r0 = bb * T_PAD
            s0 = rs_ref[bb, h]
            oint_ref[r0:r0 + T_PAD, 0:DV_B] = _mm(qd[r0:r0 + T_PAD, h * DK_B:(h + 1) * DK_B], s0)
            rso_ref[bb, h] = (s0 * scal_ref[l, H_A + h]
                              + _mm(kd_t[h * DK_B:(h + 1) * DK_B] * _col_mask(BLK, T_PAD, bb), v))
        o = o_intra + oint_ref[:, 0:DV_B]
        c0 = H_A * HD_A + h * DV_B
        mix_ref[:, c0:c0 + DV_B] = _group_norm_gate(o, p[:, C_GB + h * DV_B:C_GB + (h + 1) * DV_B])

    o_diag, qe, k2, ebl = _hgrn_front(p, lb_ref[0], padmask_ref[...], tril_ref[...], sel_ref[...], onesbd_ref[...],
                                      b_ref, q_ref, T_PAD)
    k2_t, ebl_t = k2.T, ebl.T
    v = p[:, C_IC:C_IC + HC]
    for bb in range(GB):
        r0 = bb * T_PAD
        k2m = k2_t * _col_mask(BLK, T_PAD, bb)
        for h in range(H_C):
            s0 = hs_ref[bb, h]
            oint_ref[r0:r0 + T_PAD, h * DV_C:(h + 1) * DV_C] = _mm(qe[r0:r0 + T_PAD, h * DK_C:(h + 1) * DK_C], s0)
            hso_ref[bb, h] = (ebl_t[h * DK_C:(h + 1) * DK_C, r0:r0 + 1] * s0
                              + _mm(k2m[h * DK_C:(h + 1) * DK_C], v[:, h * DV_C:(h + 1) * DV_C]))
    o = oint_ref[...] + o_diag
    c0 = H_A * HD_A + H_B * DV_B
    mix_ref[:, c0:c0 + HC] = _hgrn_finish(o, p, gain_ref[0], bdf_ref[...])


def _mixer_sample(proj8, scal, kbuf, vbuf, rstate, hstate, lb, gain, l, bs, tabs, htab):
    nb = bs // GB
    full = lambda a: pl.BlockSpec(a.shape, (lambda i: (0, 0)) if a.ndim == 2 else (lambda i: (0, 0, 0)))
    layer3 = lambda i: (l, 0, 0)
    return pl.pallas_call(
        functools.partial(_mixer_sample_kernel, l=l),
        grid=(nb,),
        in_specs=[pl.BlockSpec(memory_space=pltpu.SMEM),
                  pl.BlockSpec((BLK, D_PROJ), lambda i: (i, 0)),
                  pl.BlockSpec((GB, WINDOW, KV_A), lambda i: (l * nb + i, 0, 0)),
                  pl.BlockSpec((GB, WINDOW, KV_A), lambda i: (l * nb + i, 0, 0)),
                  pl.BlockSpec((GB, H_B, DK_B, DV_B), lambda i: (l * nb + i, 0, 0, 0)),
                  pl.BlockSpec((GB, H_C, DK_C, DV_C), lambda i: (l * nb + i, 0, 0, 0)),
                  full(tabs["bias"]), full(tabs["dblk"]), full(tabs["qdec"]), full(tabs["kdec"]),
                  pl.BlockSpec((1, 1, HC), layer3), pl.BlockSpec((1, 1, HC), layer3), full(tabs["padmask"]),
                  full(htab["tril"]), full(htab["sel"]), full(htab["onesbd"]), full(htab["bdf"])],
        out_specs=[pl.BlockSpec((BLK, D_MODEL), lambda i: (i, 0)),
                   pl.BlockSpec((GB, H_B, DK_B, DV_B), lambda i: (i, 0, 0, 0)),
                   pl.BlockSpec((GB, H_C, DK_C, DV_C), lambda i: (i, 0, 0, 0))],
        out_shape=[jax.ShapeDtypeStruct((bs * T_PAD, D_MODEL), F32),
                   jax.ShapeDtypeStruct((bs, H_B, DK_B, DV_B), F32),
                   jax.ShapeDtypeStruct((bs, H_C, DK_C, DV_C), F32)],
        scratch_shapes=[pltpu.VMEM((2 * WINDOW, KV_A), F32), pltpu.VMEM((2 * WINDOW, KV_A), F32),
                        pltpu.VMEM((BLK, HC), F32), pltpu.VMEM((BLK, HC), F32), pltpu.VMEM((BLK, HC), F32)],
        compiler_params=_cparams("parallel"),
        name="mixer_sample",
    )(scal, proj8, kbuf, vbuf, rstate, hstate, tabs["bias"], tabs["dblk"], tabs["qdec"], tabs["kdec"],
      lb, gain, tabs["padmask"], htab["tril"], htab["sel"], htab["onesbd"], htab["bdf"])


def _attend_heads(q, mk_heads, mv_heads):
    outs = []
    for h in range(XH):
        s = _mm_nt(q[:, h * XHD:(h + 1) * XHD], mk_heads[h]) * (XHD ** -0.5)
        e = jnp.exp(s - jnp.max(s, axis=-1, keepdims=True))
        outs.append(_mm(e, mv_heads[h]) / jnp.sum(e, axis=-1, keepdims=True))
    return jnp.concatenate(outs, axis=-1)


def _attend_memory(q, mk, mv):
    return _attend_heads(q, [mk[:, h * XHD:(h + 1) * XHD] for h in range(XH)],
                         [mv[:, h * XHD:(h + 1) * XHD] for h in range(XH)])


def _xattn_prompt_kernel(x_ref, g_ref, wq_ref, wo_ref, mk_ref, mv_ref, o_ref):
    x = x_ref[...]
    q = jnp.dot(_rms(x, g_ref[0]).astype(BF16), wq_ref[0], preferred_element_type=F32)
    o = _attend_memory(q, mk_ref[...], mv_ref[...])
    o_ref[...] = x + jnp.dot(o.astype(BF16), wo_ref[0], preferred_element_type=F32)


def _xattn_prompt(x, g, wq, wo, mk, mv, l, bp, t_len):
    tq = _pick(t_len, (512, 256, 128))
    nt = t_len // tq
    dq = XH * XHD
    layer3 = lambda b, t: (l, 0, 0)
    return pl.pallas_call(
        _xattn_prompt_kernel,
        grid=(bp, nt),
        in_specs=[pl.BlockSpec((tq, D_MODEL), lambda b, t: (b * nt + t, 0)),
                  pl.BlockSpec((1, 1, D_MODEL), layer3),
                  pl.BlockSpec((1, D_MODEL, dq), layer3),
                  pl.BlockSpec((1, dq, D_MODEL), layer3),
                  pl.BlockSpec((N_MEM, dq), lambda b, t: (b, 0)),
                  pl.BlockSpec((N_MEM, dq), lambda b, t: (b, 0))],
        out_specs=pl.BlockSpec((tq, D_MODEL), lambda b, t: (b * nt + t, 0)),
        out_shape=jax.ShapeDtypeStruct(x.shape, F32),
        input_output_aliases={0: 0},
        compiler_params=_cparams("parallel", "parallel"),
        name="xattn_prompt",
    )(x, g, wq, wo, mk, mv)


XB = 4


def _xattn_sample_kernel(q_ref, mk_ref, mv_ref, o_ref):
    for bb in range(XB):
        r0 = bb * T_PAD
        o_ref[r0:r0 + T_PAD] = _attend_heads(q_ref[r0:r0 + T_PAD],
                                             [mk_ref[bb, :, h, :] for h in range(XH)],
                                             [mv_ref[bb, :, h, :] for h in range(XH)])


def _xattn_sample(q8, mk, mv, l, bs):
    dq = XH * XHD
    nb = bs // XB
    return pl.pallas_call(
        _xattn_sample_kernel,
        grid=(nb,),
        in_specs=[pl.BlockSpec((XB * T_PAD, dq), lambda i: (i, 0)),
                  pl.BlockSpec((XB, N_MEM, XH, XHD), lambda i: (l * nb + i, 0, 0, 0)),
                  pl.BlockSpec((XB, N_MEM, XH, XHD), lambda i: (l * nb + i, 0, 0, 0))],
        out_specs=pl.BlockSpec((XB * T_PAD, dq), lambda i: (i, 0)),
        out_shape=jax.ShapeDtypeStruct((bs * T_PAD, dq), F32),
        compiler_params=_cparams("parallel"),
        name="xattn_sample",
    )(q8, mk, mv)


def _ffn_dense_kernel(x_ref, g_ref, w1_ref, w3_ref, w2_ref, o_ref, xn_ref):
    @pl.when(pl.program_id(1) == 0)
    def _():
        x = x_ref[...]
        xn_ref[...] = _rms(x, g_ref[0]).astype(BF16)
        o_ref[...] = x

    xn = xn_ref[...]
    h1 = jnp.dot(xn, w1_ref[0], preferred_element_type=F32)
    h3 = jnp.dot(xn, w3_ref[0], preferred_element_type=F32)
    o_ref[...] += jnp.dot((_silu(h1) * h3).astype(BF16), w2_ref[0], preferred_element_type=F32)


def _ffn_dense(x, g, l, w1, w3, w2, e):
    rows = x.shape[0]
    dff = w1.shape[2]
    tm = _pick(rows, (1536, 768) + ROW_TILES)
    tf = _pick(dff, (512, 256, 128))
    return pl.pallas_call(
        _ffn_dense_kernel,
        grid=(rows // tm, dff // tf),
        in_specs=[pl.BlockSpec((tm, D_MODEL), lambda i, j: (i, 0)),
                  pl.BlockSpec((1, 1, D_MODEL), lambda i, j: (l, 0, 0)),
                  pl.BlockSpec((1, D_MODEL, tf), lambda i, j: (e, 0, j)),
                  pl.BlockSpec((1, D_MODEL, tf), lambda i, j: (e, 0, j)),
                  pl.BlockSpec((1, tf, D_MODEL), lambda i, j: (e, j, 0))],
        out_specs=pl.BlockSpec((tm, D_MODEL), lambda i, j: (i, 0)),
        out_shape=jax.ShapeDtypeStruct((rows, D_MODEL), F32),
        scratch_shapes=[pltpu.VMEM((tm, D_MODEL), BF16)],
        input_output_aliases={0: 0},
        compiler_params=_cparams("parallel", "arbitrary"),
        name="ffn_dense",
    )(x, g, w1, w3, w2)


def _router_kernel(x_ref, g_ref, wr_ref, h_ref, idx_ref, gate_ref):
    h = _rms(x_ref[...], g_ref[0])
    h_ref[...] = h
    logits = _mm_hi(h, wr_ref[...])
    lane = lax.broadcasted_iota(I32, logits.shape, 1)
    logits = jnp.where(lane < N_EXPERTS, logits, NEG_INF)
    m1 = jnp.max(logits, axis=-1, keepdims=True)
    i1 = jnp.min(jnp.where(logits == m1, lane, LANES), axis=-1, keepdims=True)
    rest = jnp.where(lane == i1, NEG_INF, logits)
    m2 = jnp.max(rest, axis=-1, keepdims=True)
    i2 = jnp.min(jnp.where(rest == m2, lane, LANES), axis=-1, keepdims=True)
    e2 = jnp.exp(m2 - m1)
    den = 1.0 + e2
    idx_ref[...] = jnp.where(lane == 0, i1, jnp.where(lane == 1, i2, 0))
    gate_ref[...] = jnp.where(lane == 0, 1.0 / den, jnp.where(lane == 1, e2 / den, 0.0))


def _router(x, g, l, w_router):
    n = x.shape[0]
    tm = _pick(n, ROW_TILES)
    wr = jnp.pad(w_router, ((0, 0), (0, LANES - N_EXPERTS)))
    return pl.pallas_call(
        _router_kernel,
        grid=(n // tm,),
        in_specs=[pl.BlockSpec((tm, D_MODEL), lambda i: (i, 0)),
                  pl.BlockSpec((1, 1, D_MODEL), lambda i: (l, 0, 0)),
                  pl.BlockSpec((D_MODEL, LANES), lambda i: (0, 0))],
        out_specs=[pl.BlockSpec((tm, D_MODEL), lambda i: (i, 0)),
                   pl.BlockSpec((tm, LANES), lambda i: (i, 0)),
                   pl.BlockSpec((tm, LANES), lambda i: (i, 0))],
        out_shape=[jax.ShapeDtypeStruct((n, D_MODEL), F32),
                   jax.ShapeDtypeStruct((n, LANES), I32),
                   jax.ShapeDtypeStruct((n, LANES), F32)],
        compiler_params=_cparams("parallel"),
        name="router",
    )(x, g, wr)


GATHER_UNROLL = 16


def _expert_ffn_kernel(be_ref, nused_ref, tok_ref, h_ref, w1_ref, w3_ref, w2_ref, o_ref, xbuf_ref, xn_ref, sem,
                       *, tm, nf, nchunk):
    i, j = pl.program_id(0), pl.program_id(1)
    nu = nused_ref[0]
    slot = i % 2
    chunk = tm // nchunk

    def start_gather(blk, slot_, first_chunk, count):
        base = blk * tm

        def issue(k, c):
            r = first_chunk * chunk + k
            pltpu.make_async_copy(h_ref.at[pl.ds(tok_ref[base + r], 1)], xbuf_ref.at[slot_, pl.ds(r, 1)],
                                  sem.at[slot_]).start()
            return c

        lax.fori_loop(0, count * chunk, issue, 0, unroll=GATHER_UNROLL)

    first = nchunk - (nf - 1) if nchunk > 1 else 1

    if nchunk > 1:
        @pl.when((j > 0) & (i + 1 < nu))
        def _():
            start_gather(i + 1, 1 - slot, first - 1 + j, 1)

    @pl.when(j == 0)
    def _():
        @pl.when((i == 0) & (nu > 0))
        def _():
            start_gather(0, 0, 0, nchunk)

        @pl.when(i + 1 < nu)
        def _():
            start_gather(i + 1, 1 - slot, 0, first)

        @pl.when(i < nu)
        def _():
            pltpu.make_async_copy(h_ref.at[pl.ds(0, tm)], xbuf_ref.at[slot], sem.at[slot]).wait()
            xn_ref[...] = xbuf_ref[slot].astype(BF16)

        o_ref[...] = jnp.zeros_like(o_ref)

    @pl.when(i < nu)
    def _():
        xn = xn_ref[...]
        h1 = jnp.dot(xn, w1_ref[0].astype(BF16), preferred_element_type=F32)
        h3 = jnp.dot(xn, w3_ref[0].astype(BF16), preferred_element_type=F32)
        o_ref[...] += jnp.dot((_silu(h1) * h3).astype(BF16), w2_ref[0].astype(BF16), preferred_element_type=F32)


def _expert_ffn(h, tok, w1, w3, w2, block_e, nused, tm):
    rows = tok.shape[0]
    dff = w1.shape[2]
    tf = _pick(dff, (512, 256, 128))
    nf = dff // tf
    nchunk = nf + 1 if tm % ((nf + 1) * GATHER_UNROLL) == 0 else 1

    def w13map(i, j, be, nu, tok):
        return (be[jnp.minimum(i, nu[0] - 1)], 0, jnp.where(i < nu[0], j, nf - 1))

    def w2map(i, j, be, nu, tok):
        return (be[jnp.minimum(i, nu[0] - 1)], jnp.where(i < nu[0], j, nf - 1), 0)

    grid_spec = pltpu.PrefetchScalarGridSpec(
        num_scalar_prefetch=3,
        grid=(rows // tm, nf),
        in_specs=[pl.BlockSpec(memory_space=pl.ANY),
                  pl.BlockSpec((1, D_MODEL, tf), w13map),
                  pl.BlockSpec((1, D_MODEL, tf), w13map),
                  pl.BlockSpec((1, tf, D_MODEL), w2map)],
        out_specs=pl.BlockSpec((tm, D_MODEL), lambda i, j, be, nu, tok: (i, 0)),
        scratch_shapes=[pltpu.VMEM((2, tm, D_MODEL), F32), pltpu.VMEM((tm, D_MODEL), BF16),
                        pltpu.SemaphoreType.DMA((2,))],
    )
    return pl.pallas_call(
        functools.partial(_expert_ffn_kernel, tm=tm, nf=nf, nchunk=nchunk),
        grid_spec=grid_spec,
        out_shape=jax.ShapeDtypeStruct((rows, D_MODEL), F32),
        compiler_params=_cparams("arbitrary", "arbitrary"),
        name="ffn_experts",
    )(block_e, nused, tok, h, w1, w3, w2)


def _combine_kernel(pos_ref, x_ref, gate_ref, y_ref, o_ref, buf_ref, sem, *, tc):
    base = pl.program_id(0) * tc

    def issue(r, c):
        for k in range(TOP_K):
            pltpu.make_async_copy(y_ref.at[pl.ds(pos_ref[(base + r) * TOP_K + k], 1)],
                                  buf_ref.at[k, pl.ds(r, 1)], sem).start()
        return c

    lax.fori_loop(0, tc, issue, 0, unroll=GATHER_UNROLL)
    for k in range(TOP_K):
        pltpu.make_async_copy(y_ref.at[pl.ds(0, tc)], buf_ref.at[k], sem).wait()
    gate = gate_ref[...]
    o_ref[...] = x_ref[...] + (buf_ref[0] * gate[:, 0:1] + buf_ref[1] * gate[:, 1:2])


def _combine(x, gate, y, pos, tc):
    n = x.shape[0]
    grid_spec = pltpu.PrefetchScalarGridSpec(
        num_scalar_prefetch=1,
        grid=(n // tc,),
        in_specs=[pl.BlockSpec((tc, D_MODEL), lambda i, pos: (i, 0)),
                  pl.BlockSpec((tc, LANES), lambda i, pos: (i, 0)),
                  pl.BlockSpec(memory_space=pl.ANY)],
        out_specs=pl.BlockSpec((tc, D_MODEL), lambda i, pos: (i, 0)),
        scratch_shapes=[pltpu.VMEM((TOP_K, tc, D_MODEL), F32), pltpu.SemaphoreType.DMA],
    )
    return pl.pallas_call(
        functools.partial(_combine_kernel, tc=tc),
        grid_spec=grid_spec,
        out_shape=jax.ShapeDtypeStruct(x.shape, F32),
        input_output_aliases={1: 0},
        compiler_params=_cparams("arbitrary"),
        name="moe_combine",
    )(pos, x, gate, y)


def _moe(x, g, l, w_router, w1, w3, w2, e0):
    n = x.shape[0]
    a = n * TOP_K
    tm = _pick(a, (1024,) + ROW_TILES)
    h, idx, gate = _router(x, g, l, w_router)
    e_flat = idx[:, :TOP_K].reshape(a)
    onehot = (e_flat[:, None] == jnp.arange(N_EXPERTS, dtype=I32)[None, :]).astype(I32)
    csum = jnp.cumsum(onehot, axis=0)
    counts = csum[-1]
    rank = jnp.sum((csum - onehot) * onehot, axis=1)
    padded = (counts + tm - 1) // tm * tm
    pend = jnp.cumsum(padded)
    pstart = pend - padded
    dest = (pstart[e_flat] + rank).astype(I32)
    nb = a // tm + N_EXPERTS
    tok = jnp.zeros((nb * tm,), I32).at[dest].set(jnp.repeat(jnp.arange(n, dtype=I32), TOP_K))
    block_start = jnp.arange(nb, dtype=I32) * tm
    block_e = jnp.minimum(jnp.sum((pend[None, :] <= block_start[:, None]).astype(I32), axis=1), N_EXPERTS - 1)
    nused = (pend[-1:] // tm).astype(I32)
    y = _expert_ffn(h, tok, w1, w3, w2, (block_e + e0).astype(I32), nused, tm)
    return _combine(x, gate, y, dest, _pick(n, (256, 128, 64, 32, 16, 8)))


def kernel(x_prompt, x_sample, cache_swa_k, cache_swa_v, state_ret, state_hgrn, cache_mem_k, cache_mem_v, mem_prompt, norm_mix, w_in, swa_sinks, hgrn_lb_logits, hgrn_norm, w_out, norm_xattn, norm_mem, wx_q, wx_k, wx_v, wx_o, norm_ffn, ffn_w1, ffn_w3, ffn_w2, moe_router, moe_w1, moe_w3, moe_w2, final_norm):
    bp, t_len, _ = x_prompt.shape
    bs, ts, _ = x_sample.shape
    depth = w_in.shape[0]
    n_p, n_s = bp * t_len, bs * ts
    n = n_p + n_s
    assert t_len % BLK == 0 and bs % GB == 0 and bs % XB == 0 and ts <= T_PAD
    assert cache_swa_k.shape[2] == WINDOW and ts < WINDOW
    dq = XH * XHD

    x = jnp.concatenate([x_prompt.reshape(n_p, D_MODEL), x_sample.reshape(n_s, D_MODEL)], axis=0)
    ptab, stab = _prompt_tables(), _sample_tables(ts)
    htab_p, htab_s = _hgrn_tables(SUB_P), _hgrn_tables(T_PAD)
    lb_p = jax.nn.softmax(hgrn_lb_logits.astype(F32), axis=0)
    lower_bounds = (jnp.cumsum(lb_p, axis=0) - lb_p[0]).reshape(depth, 1, HC)
    gains = hgrn_norm.astype(F32).reshape(depth, 1, HC)
    sinks = swa_sinks.astype(F32)
    scal_p = jnp.concatenate([sinks, jnp.broadcast_to(jnp.asarray(ptab["cdec"], F32), (depth, H_B))], axis=1)
    scal_s = jnp.concatenate([sinks, jnp.broadcast_to(jnp.asarray(stab["cdec"], F32), (depth, H_B))], axis=1)
    mem = mem_prompt.reshape(bp * N_MEM, D_MODEL)

    bf = lambda w: w.astype(BF16)
    w_in_b, w_out_b = bf(w_in), bf(w_out)
    wq_b, wk_b, wv_b, wo_b = bf(wx_q), bf(wx_k), bf(wx_v), bf(wx_o)
    f1_b, f3_b, f2_b = bf(ffn_w1), bf(ffn_w3), bf(ffn_w2)
    m1_b = moe_w1.astype(F32).reshape((-1,) + moe_w1.shape[2:])
    m3_b = moe_w3.astype(F32).reshape((-1,) + moe_w3.shape[2:])
    m2_b = moe_w2.astype(F32).reshape((-1,) + moe_w2.shape[2:])
    vec3 = lambda v: v.astype(F32).reshape(depth, 1, D_MODEL)
    n_mix, n_xat, n_mem, n_ffn = vec3(norm_mix), vec3(norm_xattn), vec3(norm_mem), vec3(norm_ffn)

    kbuf = cache_swa_k.astype(F32).reshape(depth * bs, WINDOW, KV_A)
    vbuf = cache_swa_v.astype(F32).reshape(depth * bs, WINDOW, KV_A)
    rstate = state_ret.astype(F32).reshape((depth * bs,) + state_ret.shape[2:])
    hstate = state_hgrn.astype(F32).reshape((depth * bs,) + state_hgrn.shape[2:])
    cmk = cache_mem_k.astype(F32).reshape(depth * bs, N_MEM, XH, XHD)
    cmv = cache_mem_v.astype(F32).reshape(depth * bs, N_MEM, XH, XHD)

    def pad_tokens(a):
        return jnp.pad(a.reshape(bs, ts, -1), ((0, 0), (0, T_PAD - ts), (0, 0))).reshape(bs * T_PAD, -1)

    def unpad_tokens(a):
        return a.reshape(bs, T_PAD, -1)[:, :ts].reshape(bs * ts, -1)

    outs = {k: [] for k in ("pk", "pv", "pr", "pc", "pmk", "pmv", "sk", "sv", "sr", "sc")}
    for l in range(depth):
        proj = _rms_matmul(x, n_mix, w_in_b, l)
        mix_p, kkeep, vkeep, ret_p, hg_p = _mixer_prompt(proj, scal_p, lower_bounds, gains, l, bp, t_len, ptab, htab_p)
        proj_s = lax.slice_in_dim(proj, n_p, n, axis=0)
        mix_s, ret_s, hg_s = _mixer_sample(pad_tokens(proj_s), scal_s, kbuf, vbuf, rstate, hstate,
                                           lower_bounds, gains, l, bs, stab, htab_s)
        x = _matmul_res(mix_p, w_out_b, l, x, 0)
        x = _matmul_res(unpad_tokens(mix_s), w_out_b, l, x, n_p)

        outs["pk"].append(kkeep.reshape(bp, WINDOW, KVH_A, HD_A))
        outs["pv"].append(vkeep.reshape(bp, WINDOW, KVH_A, HD_A))
        outs["pr"].append(ret_p)
        outs["pc"].append(jnp.stack([hg_p[:, h * DK_C:(h + 1) * DK_C, h * DV_C:(h + 1) * DV_C]
                                     for h in range(H_C)], axis=1))
        new_k = proj_s[:, C_KA:C_KA + KV_A].reshape(bs, ts, KVH_A, HD_A)
        new_v = proj_s[:, C_VA:C_VA + KV_A].reshape(bs, ts, KVH_A, HD_A)
        outs["sk"].append(jnp.concatenate([cache_swa_k[l][:, ts:].astype(F32), new_k], axis=1))
        outs["sv"].append(jnp.concatenate([cache_swa_v[l][:, ts:].astype(F32), new_v], axis=1))
        outs["sr"].append(ret_s)
        outs["sc"].append(hg_s)

        mk = _rms_matmul(mem, n_mem, wk_b, l)
        mv = _rms_matmul(mem, n_mem, wv_b, l)
        outs["pmk"].append(mk.reshape(bp, N_MEM, XH, XHD))
        outs["pmv"].append(mv.reshape(bp, N_MEM, XH, XHD))
        x = _xattn_prompt(x, n_xat, wq_b, wo_b, mk, mv, l, bp, t_len)
        q_s = _rms_matmul(x, n_xat, wq_b, l, row_off=n_p, rows=n_s)
        o_s = _xattn_sample(pad_tokens(q_s), cmk, cmv, l, bs)
        x = _matmul_res(unpad_tokens(o_s), wo_b, l, x, n_p)

        e = l // 2
        if l % 2 == 0:
            x = _ffn_dense(x, n_ffn, l, f1_b, f3_b, f2_b, e)
        else:
            x = _moe(x, n_ffn, l, moe_router[e], m1_b, m3_b, m2_b, e * N_EXPERTS)

    y_p = _rmsnorm_rows(x, final_norm, 0, n_p).reshape(bp, t_len, D_MODEL)
    y_s = _rmsnorm_rows(x, final_norm, n_p, n_s).reshape(bs, ts, D_MODEL)
    st = lambda k: jnp.stack(outs[k])
    return (y_p, y_s, st("pk"), st("pv"), st("pr"), st("pc"), st("pmk"), st("pmv"),
            st("sk"), st("sv"), st("sr"), st("sc"))
```

```python
import functools

import numpy as np
import jax
import jax.numpy as jnp
from jax import lax
from jax.experimental import pallas as pl
from jax.experimental.pallas import tpu as pltpu

F32, BF16, I32 = jnp.float32, jnp.bfloat16, jnp.int32
HIGHEST = lax.Precision.HIGHEST
NEG_INF = float("-inf")

D_MODEL = 1024
WINDOW = 128
H_A, KVH_A, GQ_A, HD_A = 4, 2, 2, 64
ALIBI_MAX_EXP = 8.0
H_B, DK_B, DV_B = 4, 64, 128
RET_CHUNK = 128
H_C, DK_C, DV_C = 4, 64, 64
N_MEM, XH, XHD = 256, 4, 128
N_EXPERTS, TOP_K = 8, 2
EPS = 1e-6
D_PROJ = 3072
C_QA, C_KA, C_VA = 0, 256, 384
C_QB, C_KB, C_VB, C_GB = 512, 768, 1024, 1536
C_FC, C_QC, C_IC, C_GC = 2048, 2304, 2560, 2816
KV_A = KVH_A * HD_A
HC = H_C * DK_C

BLK = 128
SUB_P = 16
T_PAD = 8
GB = BLK // T_PAD
LANES = 128
VMEM_LIMIT = 56 * 1024 * 1024
ROW_TILES = (512, 256, 128, 64, 32, 16, 8)


def _pick(n, cands):
    for c in cands:
        if n % c == 0:
            return c
    raise ValueError(f"no tile for {n}")


def _row_tile(rows, row_off):
    return _pick(int(np.gcd(rows, row_off)) if row_off else rows, ROW_TILES)


def _cparams(*sem):
    return pltpu.CompilerParams(dimension_semantics=sem, vmem_limit_bytes=VMEM_LIMIT)


def _mm(a, b):
    return jnp.dot(a.astype(BF16), b.astype(BF16), preferred_element_type=F32)


def _mm_nt(a, b):
    return lax.dot_general(a.astype(BF16), b.astype(BF16), (((1,), (1,)), ((), ())),
                           preferred_element_type=F32)


def _mm_hi(a, b):
    return jnp.dot(a, b, precision=HIGHEST, preferred_element_type=F32)


def _sigmoid(x):
    return 1.0 / (1.0 + jnp.exp(-x))


def _silu(x):
    return x * _sigmoid(x)


def _rms(x, g):
    return x * lax.rsqrt(jnp.mean(x * x, axis=-1, keepdims=True) + EPS) * g


def _rms_matmul_kernel(x_ref, g_ref, w_ref, o_ref, xn_ref):
    @pl.when(pl.program_id(1) == 0)
    def _():
        xn_ref[...] = _rms(x_ref[...], g_ref[0]).astype(BF16)

    o_ref[...] = jnp.dot(xn_ref[...], w_ref[0], preferred_element_type=F32)


def _rms_matmul(x, g, w, l, row_off=0, rows=None):
    n, k = x.shape
    m = w.shape[2]
    rows = n if rows is None else rows
    tm = _row_tile(rows, row_off)
    tn = _pick(m, (3072, 1536, 1024, 512, 256, 128))
    off = row_off // tm
    return pl.pallas_call(
        _rms_matmul_kernel,
        grid=(rows // tm, m // tn),
        in_specs=[pl.BlockSpec((tm, k), lambda i, j: (i + off, 0)),
                  pl.BlockSpec((1, 1, k), lambda i, j: (l, 0, 0)),
                  pl.BlockSpec((1, k, tn), lambda i, j: (l, 0, j))],
        out_specs=pl.BlockSpec((tm, tn), lambda i, j: (i, j)),
        out_shape=jax.ShapeDtypeStruct((rows, m), F32),
        scratch_shapes=[pltpu.VMEM((tm, k), BF16)],
        compiler_params=_cparams("parallel", "arbitrary"),
        name="rms_matmul",
    )(x, g, w)


def _matmul_res_kernel(a_ref, w_ref, x_ref, o_ref):
    o_ref[...] = x_ref[...] + jnp.dot(a_ref[...].astype(BF16), w_ref[0], preferred_element_type=F32)


def _matmul_res(a, w, l, x, row_off):
    rows, k = a.shape
    m = w.shape[2]
    tm = _row_tile(rows, row_off)
    off = row_off // tm
    return pl.pallas_call(
        _matmul_res_kernel,
        grid=(rows // tm,),
        in_specs=[pl.BlockSpec((tm, k), lambda i: (i, 0)),
                  pl.BlockSpec((1, k, m), lambda i: (l, 0, 0)),
                  pl.BlockSpec((tm, m), lambda i: (i + off, 0))],
        out_specs=pl.BlockSpec((tm, m), lambda i: (i + off, 0)),
        out_shape=jax.ShapeDtypeStruct(x.shape, F32),
        input_output_aliases={2: 0},
        compiler_params=_cparams("parallel"),
        name="matmul_res",
    )(a, w, x)


def _rmsnorm_kernel(x_ref, g_ref, o_ref):
    o_ref[...] = _rms(x_ref[...], g_ref[...])


def _rmsnorm_rows(x, g, row_off, rows):
    n, k = x.shape
    tm = _row_tile(rows, row_off)
    off = row_off // tm
    return pl.pallas_call(
        _rmsnorm_kernel,
        grid=(rows // tm,),
        in_specs=[pl.BlockSpec((tm, k), lambda i: (i + off, 0)),
                  pl.BlockSpec((1, k), lambda i: (0, 0))],
        out_specs=pl.BlockSpec((tm, k), lambda i: (i, 0)),
        out_shape=jax.ShapeDtypeStruct((rows, k), F32),
        compiler_params=_cparams("parallel"),
        name="final_rmsnorm",
    )(x, g.reshape(1, k))


def _retention_gamma():
    return 1.0 - np.exp2(-5.0 - np.arange(H_B, dtype=np.float64))


def _alibi_slopes():
    return np.exp2(-(ALIBI_MAX_EXP / H_A) * np.arange(1, H_A + 1, dtype=np.float64))


def _head_cols(vals_hr, width):
    return np.repeat(vals_hr.T[:, :, None], width, axis=2).reshape(vals_hr.shape[1], -1)


def _hgrn_tables(sub):
    r = np.arange(BLK)
    same = (r[:, None] // sub) == (r[None, :] // sub)
    tril = (same & (r[None, :] <= r[:, None])).astype(np.float32)
    pr = np.arange(BLK * sub)
    sel = (r[:, None] == (pr // sub)[None, :]).astype(np.float32)
    hh = np.arange(HC) // DK_C
    bd = (hh[:, None] == hh[None, :]).astype(np.float32)
    return dict(tril=jnp.asarray(tril), sel=jnp.asarray(sel, dtype=BF16),
                onesbd=jnp.asarray(bd, dtype=BF16), bdf=jnp.asarray(bd))


def _prompt_tables():
    gam = _retention_gamma()
    i = np.arange(RET_CHUNK, dtype=np.float64)
    diff = i[:, None] - i[None, :]
    dmat = np.where(diff >= 0, gam[:, None, None] ** np.maximum(diff, 0.0), 0.0)
    qdec = _head_cols(gam[:, None] ** (i + 1.0), DK_B)
    kdec = _head_cols(gam[:, None] ** (RET_CHUNK - 1.0 - i), DK_B) * (DK_B ** -0.5)
    cdec = gam ** RET_CHUNK
    slopes = _alibi_slopes().reshape(KVH_A, GQ_A)
    qi = np.tile(np.arange(WINDOW), GQ_A)[:, None]
    qg = np.repeat(np.arange(GQ_A), WINDOW)[:, None]
    j = np.arange(2 * WINDOW)[None, :]
    rel = WINDOW + qi - j
    ok = (rel >= 0) & (rel < WINDOW)
    bias = np.stack([np.where(ok, -slopes[kh][qg] * rel, NEG_INF) for kh in range(KVH_A)])
    bias_first = np.where(j < WINDOW, NEG_INF, bias)
    return dict(dmat=jnp.asarray(dmat, dtype=F32), qdec=jnp.asarray(qdec, dtype=F32),
                kdec=jnp.asarray(kdec, dtype=F32), cdec=cdec,
                bias=jnp.asarray(np.stack([bias_first, bias]), dtype=F32))


def _sample_tables(t):
    gam = _retention_gamma()
    r = np.arange(BLK)
    ti, bi = r % T_PAD, r // T_PAD
    diff = (ti[:, None] - ti[None, :]).astype(np.float64)
    same = bi[:, None] == bi[None, :]
    dblk = np.where(same & (diff >= 0), gam[:, None, None] ** np.maximum(diff, 0.0), 0.0)
    qdec = _head_cols(gam[:, None] ** (ti + 1.0), DK_B)
    kdec = _head_cols(gam[:, None] ** (t - 1.0 - ti), DK_B) * (DK_B ** -0.5)
    cdec = gam ** t
    slopes = _alibi_slopes().reshape(KVH_A, GQ_A)
    qt = np.tile(np.arange(T_PAD), GQ_A)[:, None]
    qg = np.repeat(np.arange(GQ_A), T_PAD)[:, None]
    c = np.arange(2 * WINDOW)[None, :]
    rel = qt + WINDOW - c
    ok = (rel >= 0) & (rel < WINDOW) & (c < WINDOW + t)
    bias = np.stack([np.where(ok, -slopes[kh][qg] * rel, NEG_INF) for kh in range(KVH_A)])
    padmask = np.repeat((ti < t).astype(np.float32)[:, None], HC, axis=1)
    return dict(dblk=jnp.asarray(dblk, dtype=F32), qdec=jnp.asarray(qdec, dtype=F32),
                kdec=jnp.asarray(kdec, dtype=F32), cdec=cdec, bias=jnp.asarray(bias, dtype=F32),
                padmask=jnp.asarray(padmask))


def _tile_j(x, sub):
    nsub = x.shape[0] // sub
    y = jnp.broadcast_to(x.reshape(nsub, 1, sub, x.shape[1]), (nsub, sub, sub, x.shape[1]))
    return y.reshape(nsub * sub * sub, x.shape[1])


def _row_bcast(ref, r, sub):
    return jnp.broadcast_to(ref[r:r + 1, :], (sub, ref.shape[1]))


def _hgrn_front(p, lb, padmask, tril, sel, onesbd, b_ref, q_ref, sub):
    fc, qc, ic = p[:, C_FC:C_FC + HC], p[:, C_QC:C_QC + HC], p[:, C_IC:C_IC + HC]
    f = lb + (1.0 - lb) * _sigmoid(fc)
    kk = 1.0 - f
    g = jnp.log(f)
    if padmask is not None:
        kk = kk * padmask
        g = g * padmask
    qq = _silu(qc)
    b = _mm_hi(tril, g)
    b_ref[...] = b
    q_ref[...] = qq
    jrow = lax.broadcasted_iota(I32, (sub, 1), 0)
    bi = jnp.concatenate([_row_bcast(b_ref, r, sub) for r in range(BLK)], axis=0)
    qi = jnp.concatenate([jnp.where(jrow <= r % sub, _row_bcast(q_ref, r, sub), 0.0) for r in range(BLK)], axis=0)
    w = jnp.exp(jnp.minimum(bi - _tile_j(b, sub), 0.0))
    wm = qi * _tile_j(kk, sub) * w
    ab = jnp.dot(wm.astype(BF16), onesbd, preferred_element_type=F32)
    cm = ab * _tile_j(ic, sub)
    o_diag = jnp.dot(sel, cm.astype(BF16), preferred_element_type=F32)
    blrep = jnp.concatenate([_row_bcast(b_ref, i * sub + sub - 1, sub) for i in range(BLK // sub)], axis=0)
    return o_diag, qq * jnp.exp(b), kk * jnp.exp(blrep - b), jnp.exp(blrep)


def _hgrn_finish(o, p, gain, bdf):
    ms = _mm_hi(o * o, bdf) * (1.0 / DV_C)
    return o * lax.rsqrt(ms + EPS) * gain * _silu(p[:, C_GC:C_GC + HC])


def _group_norm_gate(o, gate):
    c = o - jnp.mean(o, axis=-1, keepdims=True)
    return c * lax.rsqrt(jnp.mean(c * c, axis=-1, keepdims=True) + EPS) * _silu(gate)


def _softmax_sink_pv(s, sinkcol, v):
    m = jnp.maximum(jnp.max(s, axis=-1, keepdims=True), sinkcol)
    e = jnp.exp(s - m)
    den = jnp.sum(e, axis=-1, keepdims=True) + jnp.exp(sinkcol - m)
    return _mm(e, v) / den


def _col_mask(width, sub, idx):
    return (lax.broadcasted_iota(I32, (1, width), 1) // sub == idx).astype(F32)


def _mixer_prompt_kernel(scal_ref, p_ref, prev_ref, bias_ref, dmat_ref, qdec_ref, kdec_ref, lb_ref, gain_ref,
                         tril_ref, sel_ref, onesbd_ref, bdf_ref,
                         mix_ref, kkeep_ref, vkeep_ref, rets_ref, hgs_ref,
                         sret_ref, shg_ref, b_ref, q_ref, *, l):
    t = pl.program_id(1)

    @pl.when(t == 0)
    def _():
        sret_ref[...] = jnp.zeros_like(sret_ref)
        shg_ref[...] = jnp.zeros_like(shg_ref)

    p = p_ref[...]
    prev = prev_ref[...]
    half = lax.broadcasted_iota(I32, (GQ_A * BLK, 1), 0) < BLK

    for kh in range(KVH_A):
        q = jnp.concatenate([p[:, C_QA + (kh * GQ_A + g) * HD_A:C_QA + (kh * GQ_A + g + 1) * HD_A]
                             for g in range(GQ_A)], axis=0)
        kband = jnp.concatenate([prev[:, kh * HD_A:(kh + 1) * HD_A],
                                 p[:, C_KA + kh * HD_A:C_KA + (kh + 1) * HD_A]], axis=0)
        vband = jnp.concatenate([prev[:, KV_A + kh * HD_A:KV_A + (kh + 1) * HD_A],
                                 p[:, C_VA + kh * HD_A:C_VA + (kh + 1) * HD_A]], axis=0)
        s = _mm_nt(q, kband) * (HD_A ** -0.5) + bias_ref[0, kh]
        sinkcol = jnp.where(half, scal_ref[l, kh * GQ_A], scal_ref[l, kh * GQ_A + 1])
        o = _softmax_sink_pv(s, sinkcol, vband)
        for g in range(GQ_A):
            c0 = (kh * GQ_A + g) * HD_A
            mix_ref[:, c0:c0 + HD_A] = o[g * BLK:(g + 1) * BLK]

    kd_t = (p[:, C_KB:C_KB + H_B * DK_B] * kdec_ref[...]).T
    qd = p[:, C_QB:C_QB + H_B * DK_B] * qdec_ref[...]
    for h in range(H_B):
        q = p[:, C_QB + h * DK_B:C_QB + (h + 1) * DK_B]
        k = p[:, C_KB + h * DK_B:C_KB + (h + 1) * DK_B] * (DK_B ** -0.5)
        v = p[:, C_VB + h * DV_B:C_VB + (h + 1) * DV_B]
        a = _mm_nt(q, k) * dmat_ref[h]
        s0 = sret_ref[h]
        o = _mm(a, v) + _mm(qd[:, h * DK_B:(h + 1) * DK_B], s0)
        sret_ref[h] = s0 * scal_ref[l, H_A + h] + _mm(kd_t[h * DK_B:(h + 1) * DK_B], v)
        c0 = H_A * HD_A + h * DV_B
        mix_ref[:, c0:c0 + DV_B] = _group_norm_gate(o, p[:, C_GB + h * DV_B:C_GB + (h + 1) * DV_B])

    o_diag, qe, k2, ebl = _hgrn_front(p, lb_ref[0], None, tril_ref[...], sel_ref[...], onesbd_ref[...],
                                      b_ref, q_ref, SUB_P)
    k2_t, ebl_t = k2.T, ebl.T
    v = p[:, C_IC:C_IC + HC]
    bdf = bdf_ref[...]
    st = shg_ref[...]
    o_inter = []
    for i in range(BLK // SUB_P):
        o_inter.append(_mm(qe[i * SUB_P:(i + 1) * SUB_P], st))
        u = _mm(k2_t * _col_mask(BLK, SUB_P, i), v) * bdf
        st = ebl_t[:, i * SUB_P:i * SUB_P + 1] * st + u
    shg_ref[...] = st
    o = jnp.concatenate(o_inter, axis=0) + o_diag
    c0 = H_A * HD_A + H_B * DV_B
    mix_ref[:, c0:c0 + HC] = _hgrn_finish(o, p, gain_ref[0], bdf)

    @pl.when(t == pl.num_programs(1) - 1)
    def _():
        kkeep_ref[0] = p[:, C_KA:C_KA + KV_A]
        vkeep_ref[0] = p[:, C_VA:C_VA + KV_A]
        rets_ref[0] = sret_ref[...]
        hgs_ref[0] = shg_ref[...]


def _mixer_prompt(proj, scal, lb, gain, l, bp, t_len, tabs, htab):
    nb = t_len // BLK
    const2 = lambda b, t: (0, 0)
    const3 = lambda b, t: (0, 0, 0)
    layer3 = lambda b, t: (l, 0, 0)
    full = lambda a: pl.BlockSpec(a.shape, const2 if a.ndim == 2 else const3)
    return pl.pallas_call(
        functools.partial(_mixer_prompt_kernel, l=l),
        grid=(bp, nb),
        in_specs=[pl.BlockSpec(memory_space=pltpu.SMEM),
                  pl.BlockSpec((BLK, D_PROJ), lambda b, t: (b * nb + t, 0)),
                  pl.BlockSpec((BLK, 2 * KV_A), lambda b, t: (b * nb + jnp.maximum(t - 1, 0), 1)),
                  pl.BlockSpec((1, KVH_A, GQ_A * BLK, 2 * BLK), lambda b, t: (jnp.minimum(t, 1), 0, 0, 0)),
                  full(tabs["dmat"]), full(tabs["qdec"]), full(tabs["kdec"]),
                  pl.BlockSpec((1, 1, HC), layer3), pl.BlockSpec((1, 1, HC), layer3),
                  full(htab["tril"]), full(htab["sel"]), full(htab["onesbd"]), full(htab["bdf"])],
        out_specs=[pl.BlockSpec((BLK, D_MODEL), lambda b, t: (b * nb + t, 0)),
                   pl.BlockSpec((1, WINDOW, KV_A), lambda b, t: (b, 0, 0)),
                   pl.BlockSpec((1, WINDOW, KV_A), lambda b, t: (b, 0, 0)),
                   pl.BlockSpec((1, H_B, DK_B, DV_B), lambda b, t: (b, 0, 0, 0)),
                   pl.BlockSpec((1, HC, HC), lambda b, t: (b, 0, 0))],
        out_shape=[jax.ShapeDtypeStruct((bp * t_len, D_MODEL), F32),
                   jax.ShapeDtypeStruct((bp, WINDOW, KV_A), F32),
                   jax.ShapeDtypeStruct((bp, WINDOW, KV_A), F32),
                   jax.ShapeDtypeStruct((bp, H_B, DK_B, DV_B), F32),
                   jax.ShapeDtypeStruct((bp, HC, HC), F32)],
        scratch_shapes=[pltpu.VMEM((H_B, DK_B, DV_B), F32), pltpu.VMEM((HC, HC), F32),
                        pltpu.VMEM((BLK, HC), F32), pltpu.VMEM((BLK, HC), F32)],
        compiler_params=_cparams("parallel", "arbitrary"),
        name="mixer_prompt",
    )(scal, proj, proj, tabs["bias"], tabs["dmat"], tabs["qdec"], tabs["kdec"], lb, gain,
      htab["tril"], htab["sel"], htab["onesbd"], htab["bdf"])


def _mixer_sample_kernel(scal_ref, p_ref, kbuf_ref, vbuf_ref, rs_ref, hs_ref, bias_ref, dblk_ref, qdec_ref,
                         kdec_ref, lb_ref, gain_ref, padmask_ref, tril_ref, sel_ref, onesbd_ref, bdf_ref,
                         mix_ref, rso_ref, hso_ref, kall_ref, vall_ref, oint_ref, b_ref, q_ref, *, l):
    p = p_ref[...]
    half = lax.broadcasted_iota(I32, (GQ_A * T_PAD, 1), 0) < T_PAD
    kall_ref[...] = jnp.zeros_like(kall_ref)
    vall_ref[...] = jnp.zeros_like(vall_ref)

    for bb in range(GB):
        r0 = bb * T_PAD
        kall_ref[0:WINDOW] = kbuf_ref[bb]
        vall_ref[0:WINDOW] = vbuf_ref[bb]
        kall_ref[WINDOW:WINDOW + T_PAD] = p[r0:r0 + T_PAD, C_KA:C_KA + KV_A]
        vall_ref[WINDOW:WINDOW + T_PAD] = p[r0:r0 + T_PAD, C_VA:C_VA + KV_A]
        kall, vall = kall_ref[...], vall_ref[...]
        for kh in range(KVH_A):
            q = jnp.concatenate([p[r0:r0 + T_PAD, C_QA + (kh * GQ_A + g) * HD_A:C_QA + (kh * GQ_A + g + 1) * HD_A]
                                 for g in range(GQ_A)], axis=0)
            s = _mm_nt(q, kall[:, kh * HD_A:(kh + 1) * HD_A]) * (HD_A ** -0.5) + bias_ref[kh]
            sinkcol = jnp.where(half, scal_ref[l, kh * GQ_A], scal_ref[l, kh * GQ_A + 1])
            o = _softmax_sink_pv(s, sinkcol, vall[:, kh * HD_A:(kh + 1) * HD_A])
            for g in range(GQ_A):
                c0 = (kh * GQ_A + g) * HD_A
                mix_ref[r0:r0 + T_PAD, c0:c0 + HD_A] = o[g * T_PAD:(g + 1) * T_PAD]

    kd_t = (p[:, C_KB:C_KB + H_B * DK_B] * kdec_ref[...]).T
    qd = p[:, C_QB:C_QB + H_B * DK_B] * qdec_ref[...]
    for h in range(H_B):
        q = p[:, C_QB + h * DK_B:C_QB + (h + 1) * DK_B]
        k = p[:, C_KB + h * DK_B:C_KB + (h + 1) * DK_B] * (DK_B ** -0.5)
        v = p[:, C_VB + h * DV_B:C_VB + (h + 1) * DV_B]
        a = _mm_nt(q, k) * dblk_ref[h]
        o_intra = _mm(a, v)
        for bb in range(GB):
            r0 = bb * T_PAD
            s0 = rs_ref[bb, h]
            oint_ref[r0:r0 + T_PAD, 0:DV_B] = _mm(qd[r0:r0 + T_PAD, h * DK_B:(h + 1) * DK_B], s0)
            rso_ref[bb, h] = (s0 * scal_ref[l, H_A + h]
                              + _mm(kd_t[h * DK_B:(h + 1) * DK_B] * _col_mask(BLK, T_PAD, bb), v))
        o = o_intra + oint_ref[:, 0:DV_B]
        c0 = H_A * HD_A + h * DV_B
        mix_ref[:, c0:c0 + DV_B] = _group_norm_gate(o, p[:, C_GB + h * DV_B:C_GB + (h + 1) * DV_B])

    o_diag, qe, k2, ebl = _hgrn_front(p, lb_ref[0], padmask_ref[...], tril_ref[...], sel_ref[...], onesbd_ref[...],
                                      b_ref, q_ref, T_PAD)
    k2_t, ebl_t = k2.T, ebl.T
    v = p[:, C_IC:C_IC + HC]
    for bb in range(GB):
        r0 = bb * T_PAD
        k2m = k2_t * _col_mask(BLK, T_PAD, bb)
        for h in range(H_C):
            s0 = hs_ref[bb, h]
            oint_ref[r0:r0 + T_PAD, h * DV_C:(h + 1) * DV_C] = _mm(qe[r0:r0 + T_PAD, h * DK_C:(h + 1) * DK_C], s0)
            hso_ref[bb, h] = (ebl_t[h * DK_C:(h + 1) * DK_C, r0:r0 + 1] * s0
                              + _mm(k2m[h * DK_C:(h + 1) * DK_C], v[:, h * DV_C:(h + 1) * DV_C]))
    o = oint_ref[...] + o_diag
    c0 = H_A * HD_A + H_B * DV_B
    mix_ref[:, c0:c0 + HC] = _hgrn_finish(o, p, gain_ref[0], bdf_ref[...])


def _mixer_sample(proj8, scal, kbuf, vbuf, rstate, hstate, lb, gain, l, bs, tabs, htab):
    nb = bs // GB
    full = lambda a: pl.BlockSpec(a.shape, (lambda i: (0, 0)) if a.ndim == 2 else (lambda i: (0, 0, 0)))
    layer3 = lambda i: (l, 0, 0)
    return pl.pallas_call(
        functools.partial(_mixer_sample_kernel, l=l),
        grid=(nb,),
        in_specs=[pl.BlockSpec(memory_space=pltpu.SMEM),
                  pl.BlockSpec((BLK, D_PROJ), lambda i: (i, 0)),
                  pl.BlockSpec((GB, WINDOW, KV_A), lambda i: (l * nb + i, 0, 0)),
                  pl.BlockSpec((GB, WINDOW, KV_A), lambda i: (l * nb + i, 0, 0)),
                  pl.BlockSpec((GB, H_B, DK_B, DV_B), lambda i: (l * nb + i, 0, 0, 0)),
                  pl.BlockSpec((GB, H_C, DK_C, DV_C), lambda i: (l * nb + i, 0, 0, 0)),
                  full(tabs["bias"]), full(tabs["dblk"]), full(tabs["qdec"]), full(tabs["kdec"]),
                  pl.BlockSpec((1, 1, HC), layer3), pl.BlockSpec((1, 1, HC), layer3), full(tabs["padmask"]),
                  full(htab["tril"]), full(htab["sel"]), full(htab["onesbd"]), full(htab["bdf"])],
        out_specs=[pl.BlockSpec((BLK, D_MODEL), lambda i: (i, 0)),
                   pl.BlockSpec((GB, H_B, DK_B, DV_B), lambda i: (i, 0, 0, 0)),
                   pl.BlockSpec((GB, H_C, DK_C, DV_C), lambda i: (i, 0, 0, 0))],
        out_shape=[jax.ShapeDtypeStruct((bs * T_PAD, D_MODEL), F32),
                   jax.ShapeDtypeStruct((bs, H_B, DK_B, DV_B), F32),
                   jax.ShapeDtypeStruct((bs, H_C, DK_C, DV_C), F32)],
        scratch_shapes=[pltpu.VMEM((2 * WINDOW, KV_A), F32), pltpu.VMEM((2 * WINDOW, KV_A), F32),
                        pltpu.VMEM((BLK, HC), F32), pltpu.VMEM((BLK, HC), F32), pltpu.VMEM((BLK, HC), F32)],
        compiler_params=_cparams("parallel"),
        name="mixer_sample",
    )(scal, proj8, kbuf, vbuf, rstate, hstate, tabs["bias"], tabs["dblk"], tabs["qdec"], tabs["kdec"],
      lb, gain, tabs["padmask"], htab["tril"], htab["sel"], htab["onesbd"], htab["bdf"])


def _attend_heads(q, mk_heads, mv_heads):
    outs = []
    for h in range(XH):
        s = _mm_nt(q[:, h * XHD:(h + 1) * XHD], mk_heads[h]) * (XHD ** -0.5)
        e = jnp.exp(s - jnp.max(s, axis=-1, keepdims=True))
        outs.append(_mm(e, mv_heads[h]) / jnp.sum(e, axis=-1, keepdims=True))
    return jnp.concatenate(outs, axis=-1)


def _attend_memory(q, mk, mv):
    return _attend_heads(q, [mk[:, h * XHD:(h + 1) * XHD] for h in range(XH)],
                         [mv[:, h * XHD:(h + 1) * XHD] for h in range(XH)])


def _xattn_prompt_kernel(x_ref, g_ref, wq_ref, wo_ref, mk_ref, mv_ref, o_ref):
    x = x_ref[...]
    q = jnp.dot(_rms(x, g_ref[0]).astype(BF16), wq_ref[0], preferred_element_type=F32)
    o = _attend_memory(q, mk_ref[...], mv_ref[...])
    o_ref[...] = x + jnp.dot(o.astype(BF16), wo_ref[0], preferred_element_type=F32)


def _xattn_prompt(x, g, wq, wo, mk, mv, l, bp, t_len):
    tq = _pick(t_len, (512, 256, 128))
    nt = t_len // tq
    dq = XH * XHD
    layer3 = lambda b, t: (l, 0, 0)
    return pl.pallas_call(
        _xattn_prompt_kernel,
        grid=(bp, nt),
        in_specs=[pl.BlockSpec((tq, D_MODEL), lambda b, t: (b * nt + t, 0)),
                  pl.BlockSpec((1, 1, D_MODEL), layer3),
                  pl.BlockSpec((1, D_MODEL, dq), layer3),
                  pl.BlockSpec((1, dq, D_MODEL), layer3),
                  pl.BlockSpec((N_MEM, dq), lambda b, t: (b, 0)),
                  pl.BlockSpec((N_MEM, dq), lambda b, t: (b, 0))],
        out_specs=pl.BlockSpec((tq, D_MODEL), lambda b, t: (b * nt + t, 0)),
        out_shape=jax.ShapeDtypeStruct(x.shape, F32),
        input_output_aliases={0: 0},
        compiler_params=_cparams("parallel", "parallel"),
        name="xattn_prompt",
    )(x, g, wq, wo, mk, mv)


XB = 4


def _xattn_sample_kernel(q_ref, mk_ref, mv_ref, o_ref):
    for bb in range(XB):
        r0 = bb * T_PAD
        o_ref[r0:r0 + T_PAD] = _attend_heads(q_ref[r0:r0 + T_PAD],
                                             [mk_ref[bb, :, h, :] for h in range(XH)],
                                             [mv_ref[bb, :, h, :] for h in range(XH)])


def _xattn_sample(q8, mk, mv, l, bs):
    dq = XH * XHD
    nb = bs // XB
    return pl.pallas_call(
        _xattn_sample_kernel,
        grid=(nb,),
        in_specs=[pl.BlockSpec((XB * T_PAD, dq), lambda i: (i, 0)),
                  pl.BlockSpec((XB, N_MEM, XH, XHD), lambda i: (l * nb + i, 0, 0, 0)),
                  pl.BlockSpec((XB, N_MEM, XH, XHD), lambda i: (l * nb + i, 0, 0, 0))],
        out_specs=pl.BlockSpec((XB * T_PAD, dq), lambda i: (i, 0)),
        out_shape=jax.ShapeDtypeStruct((bs * T_PAD, dq), F32),
        compiler_params=_cparams("parallel"),
        name="xattn_sample",
    )(q8, mk, mv)


def _ffn_dense_kernel(x_ref, g_ref, w1_ref, w3_ref, w2_ref, o_ref, xn_ref):
    @pl.when(pl.program_id(1) == 0)
    def _():
        x = x_ref[...]
        xn_ref[...] = _rms(x, g_ref[0]).astype(BF16)
        o_ref[...] = x

    xn = xn_ref[...]
    h1 = jnp.dot(xn, w1_ref[0], preferred_element_type=F32)
    h3 = jnp.dot(xn, w3_ref[0], preferred_element_type=F32)
    o_ref[...] += jnp.dot((_silu(h1) * h3).astype(BF16), w2_ref[0], preferred_element_type=F32)


def _ffn_dense(x, g, l, w1, w3, w2, e):
    rows = x.shape[0]
    dff = w1.shape[2]
    tm = _pick(rows, (1536, 768) + ROW_TILES)
    tf = _pick(dff, (512, 256, 128))
    return pl.pallas_call(
        _ffn_dense_kernel,
        grid=(rows // tm, dff // tf),
        in_specs=[pl.BlockSpec((tm, D_MODEL), lambda i, j: (i, 0)),
                  pl.BlockSpec((1, 1, D_MODEL), lambda i, j: (l, 0, 0)),
                  pl.BlockSpec((1, D_MODEL, tf), lambda i, j: (e, 0, j)),
                  pl.BlockSpec((1, D_MODEL, tf), lambda i, j: (e, 0, j)),
                  pl.BlockSpec((1, tf, D_MODEL), lambda i, j: (e, j, 0))],
        out_specs=pl.BlockSpec((tm, D_MODEL), lambda i, j: (i, 0)),
        out_shape=jax.ShapeDtypeStruct((rows, D_MODEL), F32),
        scratch_shapes=[pltpu.VMEM((tm, D_MODEL), BF16)],
        input_output_aliases={0: 0},
        compiler_params=_cparams("parallel", "arbitrary"),
        name="ffn_dense",
    )(x, g, w1, w3, w2)


def _router_kernel(x_ref, g_ref, wr_ref, h_ref, idx_ref, gate_ref):
    h = _rms(x_ref[...], g_ref[0])
    h_ref[...] = h
    logits = _mm_hi(h, wr_ref[...])
    lane = lax.broadcasted_iota(I32, logits.shape, 1)
    logits = jnp.where(lane < N_EXPERTS, logits, NEG_INF)
    m1 = jnp.max(logits, axis=-1, keepdims=True)
    i1 = jnp.min(jnp.where(logits == m1, lane, LANES), axis=-1, keepdims=True)
    rest = jnp.where(lane == i1, NEG_INF, logits)
    m2 = jnp.max(rest, axis=-1, keepdims=True)
    i2 = jnp.min(jnp.where(rest == m2, lane, LANES), axis=-1, keepdims=True)
    e2 = jnp.exp(m2 - m1)
    den = 1.0 + e2
    idx_ref[...] = jnp.where(lane == 0, i1, jnp.where(lane == 1, i2, 0))
    gate_ref[...] = jnp.where(lane == 0, 1.0 / den, jnp.where(lane == 1, e2 / den, 0.0))


def _router(x, g, l, w_router):
    n = x.shape[0]
    tm = _pick(n, ROW_TILES)
    wr = jnp.pad(w_router, ((0, 0), (0, LANES - N_EXPERTS)))
    return pl.pallas_call(
        _router_kernel,
        grid=(n // tm,),
        in_specs=[pl.BlockSpec((tm, D_MODEL), lambda i: (i, 0)),
                  pl.BlockSpec((1, 1, D_MODEL), lambda i: (l, 0, 0)),
                  pl.BlockSpec((D_MODEL, LANES), lambda i: (0, 0))],
        out_specs=[pl.BlockSpec((tm, D_MODEL), lambda i: (i, 0)),
                   pl.BlockSpec((tm, LANES), lambda i: (i, 0)),
                   pl.BlockSpec((tm, LANES), lambda i: (i, 0))],
        out_shape=[jax.ShapeDtypeStruct((n, D_MODEL), F32),
                   jax.ShapeDtypeStruct((n, LANES), I32),
                   jax.ShapeDtypeStruct((n, LANES), F32)],
        compiler_params=_cparams("parallel"),
        name="router",
    )(x, g, wr)


GATHER_UNROLL = 16


def _expert_ffn_kernel(be_ref, nused_ref, tok_ref, h_ref, w1_ref, w3_ref, w2_ref, o_ref, xbuf_ref, xn_ref, sem,
                       *, tm, nf, nchunk):
    i, j = pl.program_id(0), pl.program_id(1)
    nu = nused_ref[0]
    slot = i % 2
    chunk = tm // nchunk

    def start_gather(blk, slot_, first_chunk, count):
        base = blk * tm

        def issue(k, c):
            r = first_chunk * chunk + k
            pltpu.make_async_copy(h_ref.at[pl.ds(tok_ref[base + r], 1)], xbuf_ref.at[slot_, pl.ds(r, 1)],
                                  sem.at[slot_]).start()
            return c

        lax.fori_loop(0, count * chunk, issue, 0, unroll=GATHER_UNROLL)

    first = nchunk - (nf - 1) if nchunk > 1 else 1

    if nchunk > 1:
        @pl.when((j > 0) & (i + 1 < nu))
        def _():
            start_gather(i + 1, 1 - slot, first - 1 + j, 1)

    @pl.when(j == 0)
    def _():
        @pl.when((i == 0) & (nu > 0))
        def _():
            start_gather(0, 0, 0, nchunk)

        @pl.when(i + 1 < nu)
        def _():
            start_gather(i + 1, 1 - slot, 0, first)

        @pl.when(i < nu)
        def _():
            pltpu.make_async_copy(h_ref.at[pl.ds(0, tm)], xbuf_ref.at[slot], sem.at[slot]).wait()
            xn_ref[...] = xbuf_ref[slot].astype(BF16)

        o_ref[...] = jnp.zeros_like(o_ref)

    @pl.when(i < nu)
    def _():
        xn = xn_ref[...]
        h1 = jnp.dot(xn, w1_ref[0].astype(BF16), preferred_element_type=F32)
        h3 = jnp.dot(xn, w3_ref[0].astype(BF16), preferred_element_type=F32)
        o_ref[...] += jnp.dot((_silu(h1) * h3).astype(BF16), w2_ref[0].astype(BF16), preferred_element_type=F32)


def _expert_ffn(h, tok, w1, w3, w2, block_e, nused, tm):
    rows = tok.shape[0]
    dff = w1.shape[2]
    tf = _pick(dff, (512, 256, 128))
    nf = dff // tf
    nchunk = nf + 1 if tm % ((nf + 1) * GATHER_UNROLL) == 0 else 1

    def w13map(i, j, be, nu, tok):
        return (be[jnp.minimum(i, nu[0] - 1)], 0, jnp.where(i < nu[0], j, nf - 1))

    def w2map(i, j, be, nu, tok):
        return (be[jnp.minimum(i, nu[0] - 1)], jnp.where(i < nu[0], j, nf - 1), 0)

    grid_spec = pltpu.PrefetchScalarGridSpec(
        num_scalar_prefetch=3,
        grid=(rows // tm, nf),
        in_specs=[pl.BlockSpec(memory_space=pl.ANY),
                  pl.BlockSpec((1, D_MODEL, tf), w13map),
                  pl.BlockSpec((1, D_MODEL, tf), w13map),
                  pl.BlockSpec((1, tf, D_MODEL), w2map)],
        out_specs=pl.BlockSpec((tm, D_MODEL), lambda i, j, be, nu, tok: (i, 0)),
        scratch_shapes=[pltpu.VMEM((2, tm, D_MODEL), F32), pltpu.VMEM((tm, D_MODEL), BF16),
                        pltpu.SemaphoreType.DMA((2,))],
    )
    return pl.pallas_call(
        functools.partial(_expert_ffn_kernel, tm=tm, nf=nf, nchunk=nchunk),
        grid_spec=grid_spec,
        out_shape=jax.ShapeDtypeStruct((rows, D_MODEL), F32),
        compiler_params=_cparams("arbitrary", "arbitrary"),
        name="ffn_experts",
    )(block_e, nused, tok, h, w1, w3, w2)


def _combine_kernel(pos_ref, x_ref, gate_ref, y_ref, o_ref, buf_ref, sem, *, tc):
    base = pl.program_id(0) * tc

    def issue(r, c):
        for k in range(TOP_K):
            pltpu.make_async_copy(y_ref.at[pl.ds(pos_ref[(base + r) * TOP_K + k], 1)],
                                  buf_ref.at[k, pl.ds(r, 1)], sem).start(priority=k)
        return c

    lax.fori_loop(0, tc, issue, 0, unroll=GATHER_UNROLL)
    for k in range(TOP_K):
        pltpu.make_async_copy(y_ref.at[pl.ds(0, tc)], buf_ref.at[k], sem).wait()
    gate = gate_ref[...]
    o_ref[...] = x_ref[...] + (buf_ref[0] * gate[:, 0:1] + buf_ref[1] * gate[:, 1:2])


def _combine(x, gate, y, pos, tc):
    n = x.shape[0]
    grid_spec = pltpu.PrefetchScalarGridSpec(
        num_scalar_prefetch=1,
        grid=(n // tc,),
        in_specs=[pl.BlockSpec((tc, D_MODEL), lambda i, pos: (i, 0)),
                  pl.BlockSpec((tc, LANES), lambda i, pos: (i, 0)),
                  pl.BlockSpec(memory_space=pl.ANY)],
        out_specs=pl.BlockSpec((tc, D_MODEL), lambda i, pos: (i, 0)),
        scratch_shapes=[pltpu.VMEM((TOP_K, tc, D_MODEL), F32), pltpu.SemaphoreType.DMA],
    )
    return pl.pallas_call(
        functools.partial(_combine_kernel, tc=tc),
        grid_spec=grid_spec,
        out_shape=jax.ShapeDtypeStruct(x.shape, F32),
        input_output_aliases={1: 0},
        compiler_params=_cparams("arbitrary"),
        name="moe_combine",
    )(pos, x, gate, y)


def _moe(x, g, l, w_router, w1, w3, w2, e0):
    n = x.shape[0]
    a = n * TOP_K
    tm = _pick(a, (1024,) + ROW_TILES)
    h, idx, gate = _router(x, g, l, w_router)
    e_flat = idx[:, :TOP_K].reshape(a)
    onehot = (e_flat[:, None] == jnp.arange(N_EXPERTS, dtype=I32)[None, :]).astype(I32)
    csum = jnp.cumsum(onehot, axis=0)
    counts = csum[-1]
    rank = jnp.sum((csum - onehot) * onehot, axis=1)
    padded = (counts + tm - 1) // tm * tm
    pend = jnp.cumsum(padded)
    pstart = pend - padded
    dest = (pstart[e_flat] + rank).astype(I32)
    nb = a // tm + N_EXPERTS
    tok = jnp.zeros((nb * tm,), I32).at[dest].set(jnp.repeat(jnp.arange(n, dtype=I32), TOP_K))
    block_start = jnp.arange(nb, dtype=I32) * tm
    block_e = jnp.minimum(jnp.sum((pend[None, :] <= block_start[:, None]).astype(I32), axis=1), N_EXPERTS - 1)
    nused = (pend[-1:] // tm).astype(I32)
    y = _expert_ffn(h, tok, w1, w3, w2, (block_e + e0).astype(I32), nused, tm)
    return _combine(x, gate, y, dest, _pick(n, (256, 128, 64, 32, 16, 8)))


def kernel(x_prompt, x_sample, cache_swa_k, cache_swa_v, state_ret, state_hgrn, cache_mem_k, cache_mem_v, mem_prompt, norm_mix, w_in, swa_sinks, hgrn_lb_logits, hgrn_norm, w_out, norm_xattn, norm_mem, wx_q, wx_k, wx_v, wx_o, norm_ffn, ffn_w1, ffn_w3, ffn_w2, moe_router, moe_w1, moe_w3, moe_w2, final_norm):
    bp, t_len, _ = x_prompt.shape
    bs, ts, _ = x_sample.shape
    depth = w_in.shape[0]
    n_p, n_s = bp * t_len, bs * ts
    n = n_p + n_s
    assert t_len % BLK == 0 and bs % GB == 0 and bs % XB == 0 and ts <= T_PAD
    assert cache_swa_k.shape[2] == WINDOW and ts < WINDOW
    dq = XH * XHD

    x = jnp.concatenate([x_prompt.reshape(n_p, D_MODEL), x_sample.reshape(n_s, D_MODEL)], axis=0)
    ptab, stab = _prompt_tables(), _sample_tables(ts)
    htab_p, htab_s = _hgrn_tables(SUB_P), _hgrn_tables(T_PAD)
    lb_p = jax.nn.softmax(hgrn_lb_logits.astype(F32), axis=0)
    lower_bounds = (jnp.cumsum(lb_p, axis=0) - lb_p[0]).reshape(depth, 1, HC)
    gains = hgrn_norm.astype(F32).reshape(depth, 1, HC)
    sinks = swa_sinks.astype(F32)
    scal_p = jnp.concatenate([sinks, jnp.broadcast_to(jnp.asarray(ptab["cdec"], F32), (depth, H_B))], axis=1)
    scal_s = jnp.concatenate([sinks, jnp.broadcast_to(jnp.asarray(stab["cdec"], F32), (depth, H_B))], axis=1)
    mem = mem_prompt.reshape(bp * N_MEM, D_MODEL)

    bf = lambda w: w.astype(BF16)
    w_in_b, w_out_b = bf(w_in), bf(w_out)
    wq_b, wk_b, wv_b, wo_b = bf(wx_q), bf(wx_k), bf(wx_v), bf(wx_o)
    f1_b, f3_b, f2_b = bf(ffn_w1), bf(ffn_w3), bf(ffn_w2)
    m1_b = moe_w1.astype(F32).reshape((-1,) + moe_w1.shape[2:])
    m3_b = moe_w3.astype(F32).reshape((-1,) + moe_w3.shape[2:])
    m2_b = moe_w2.astype(F32).reshape((-1,) + moe_w2.shape[2:])
    vec3 = lambda v: v.astype(F32).reshape(depth, 1, D_MODEL)
    n_mix, n_xat, n_mem, n_ffn = vec3(norm_mix), vec3(norm_xattn), vec3(norm_mem), vec3(norm_ffn)

    kbuf = cache_swa_k.astype(F32).reshape(depth * bs, WINDOW, KV_A)
    vbuf = cache_swa_v.astype(F32).reshape(depth * bs, WINDOW, KV_A)
    rstate = state_ret.astype(F32).reshape((depth * bs,) + state_ret.shape[2:])
    hstate = state_hgrn.astype(F32).reshape((depth * bs,) + state_hgrn.shape[2:])
    cmk = cache_mem_k.astype(F32).reshape(depth * bs, N_MEM, XH, XHD)
    cmv = cache_mem_v.astype(F32).reshape(depth * bs, N_MEM, XH, XHD)

    def pad_tokens(a):
        return jnp.pad(a.reshape(bs, ts, -1), ((0, 0), (0, T_PAD - ts), (0, 0))).reshape(bs * T_PAD, -1)

    def unpad_tokens(a):
        return a.reshape(bs, T_PAD, -1)[:, :ts].reshape(bs * ts, -1)

    outs = {k: [] for k in ("pk", "pv", "pr", "pc", "pmk", "pmv", "sk", "sv", "sr", "sc")}
    for l in range(depth):
        proj = _rms_matmul(x, n_mix, w_in_b, l)
        mix_p, kkeep, vkeep, ret_p, hg_p = _mixer_prompt(proj, scal_p, lower_bounds, gains, l, bp, t_len, ptab, htab_p)
        proj_s = lax.slice_in_dim(proj, n_p, n, axis=0)
        mix_s, ret_s, hg_s = _mixer_sample(pad_tokens(proj_s), scal_s, kbuf, vbuf, rstate, hstate,
                                           lower_bounds, gains, l, bs, stab, htab_s)
        x = _matmul_res(mix_p, w_out_b, l, x, 0)
        x = _matmul_res(unpad_tokens(mix_s), w_out_b, l, x, n_p)

        outs["pk"].append(kkeep.reshape(bp, WINDOW, KVH_A, HD_A))
        outs["pv"].append(vkeep.reshape(bp, WINDOW, KVH_A, HD_A))
        outs["pr"].append(ret_p)
        outs["pc"].append(jnp.stack([hg_p[:, h * DK_C:(h + 1) * DK_C, h * DV_C:(h + 1) * DV_C]
                                     for h in range(H_C)], axis=1))
        new_k = proj_s[:, C_KA:C_KA + KV_A].reshape(bs, ts, KVH_A, HD_A)
        new_v = proj_s[:, C_VA:C_VA + KV_A].reshape(bs, ts, KVH_A, HD_A)
        outs["sk"].append(jnp.concatenate([cache_swa_k[l][:, ts:].astype(F32), new_k], axis=1))
        outs["sv"].append(jnp.concatenate([cache_swa_v[l][:, ts:].astype(F32), new_v], axis=1))
        outs["sr"].append(ret_s)
        outs["sc"].append(hg_s)

        mk = _rms_matmul(mem, n_mem, wk_b, l)
        mv = _rms_matmul(mem, n_mem, wv_b, l)
        outs["pmk"].append(mk.reshape(bp, N_MEM, XH, XHD))
        outs["pmv"].append(mv.reshape(bp, N_MEM, XH, XHD))
        x = _xattn_prompt(x, n_xat, wq_b, wo_b, mk, mv, l, bp, t_len)
        q_s = _rms_matmul(x, n_xat, wq_b, l, row_off=n_p, rows=n_s)
        o_s = _xattn_sample(pad_tokens(q_s), cmk, cmv, l, bs)
        x = _matmul_res(unpad_tokens(o_s), wo_b, l, x, n_p)

        e = l // 2
        if l % 2 == 0:
            x = _ffn_dense(x, n_ffn, l, f1_b, f3_b, f2_b, e)
        else:
            x = _moe(x, n_ffn, l, moe_router[e], m1_b, m3_b, m2_b, e * N_EXPERTS)

    y_p = _rmsnorm_rows(x, final_norm, 0, n_p).reshape(bp, t_len, D_MODEL)
    y_s = _rmsnorm_rows(x, final_norm, n_p, n_s).reshape(bs, ts, D_MODEL)
    st = lambda k: jnp.stack(outs[k])
    return (y_p, y_s, st("pk"), st("pv"), st("pr"), st("pc"), st("pmk"), st("pmv"),
            st("sk"), st("sv"), st("sr"), st("sc"))
```
